```python
import math
import jax, jax.numpy as jnp
from jax import lax
import numpy as np

D_MODEL = 1024
BATCH = 4
SEQ = 4096
DEPTH = 4

BLOCK = 128
A_GROUPS = 4
A_GROUP_DIM = 128
A_WIDTH = A_GROUPS * A_GROUP_DIM
B_HEADS = 4
B_QK_DIM = 64
B_V_DIM = 2 * B_QK_DIM
B_QK_WIDTH = B_HEADS * 2 * B_QK_DIM
B_WIDTH = B_HEADS * B_V_DIM
C_HEADS = 8
C_HEAD_DIM = 64
C_WIDTH = C_HEADS * C_HEAD_DIM
N_BRANCHES = 3
GATE_WIDTH = N_BRANCHES * D_MODEL
IN_SPLITS = (A_WIDTH, A_WIDTH, B_QK_WIDTH, B_QK_WIDTH, B_WIDTH, C_WIDTH, C_WIDTH, C_WIDTH, GATE_WIDTH)
IN_WIDTH = 2 * A_WIDTH + 2 * B_QK_WIDTH + B_WIDTH + 3 * C_WIDTH + GATE_WIDTH
N_EXPERTS = 16
N_GROUPS = 4
EXPERTS_PER_GROUP = N_EXPERTS // N_GROUPS
TOP_K = 2
D_FF_EXPERT = 512
EPS = 1e-6

kernel_name = "hybrid_gated_gmlp_diffattn_stickbreak_groupmoe"


def rms_norm(x, g):
    xf = x.astype(jnp.float32)
    y = xf * lax.rsqrt(jnp.mean(xf * xf, axis=-1, keepdims=True) + EPS)
    return (y * g.astype(jnp.float32)).astype(x.dtype)


def alibi_slopes(n):
    start = 2.0 ** (-8.0 / n)
    return jnp.asarray([start ** (i + 1) for i in range(n)], dtype=jnp.float32)


def chunked_spatial_gating(u, v, w_s, b_s, g_v):
    bsz, seq, _ = u.shape
    u = jax.nn.gelu(u)
    v = rms_norm(jax.nn.gelu(v), g_v)
    vc = v.reshape(bsz, seq // BLOCK, BLOCK, A_GROUPS, A_GROUP_DIM)
    causal = jnp.tril(jnp.ones((BLOCK, BLOCK), dtype=bool))
    w = jnp.where(causal[None], w_s, 0).astype(v.dtype)
    mixed = jnp.einsum('gts,bcsgd->bctgd', w, vc) + b_s.T.astype(v.dtype)[None, None, :, :, None]
    return u * mixed.reshape(bsz, seq, A_WIDTH)


def diff_attention(q, k, v, lam, slopes):
    bsz, seq = q.shape[:2]
    nb = seq // BLOCK
    scale = B_QK_DIM ** -0.5
    qb = jnp.moveaxis(q.reshape(bsz, nb, BLOCK, B_HEADS, 2, B_QK_DIM), 1, 0)
    key_pos = jnp.arange(seq)

    def one_block(args):
        qblk, i = args
        q_pos = i * BLOCK + jnp.arange(BLOCK)
        dist = (q_pos[:, None] - key_pos[None, :]).astype(jnp.float32)
        s = jnp.einsum('bqhmd,bkhmd->bhmqk', qblk, k).astype(jnp.float32) * scale
        s = s - slopes[None, :, None, None, None] * dist
        s = jnp.where(dist >= 0, s, -jnp.inf)
        p = jax.nn.softmax(s, axis=-1)
        a = p[:, :, 0] - lam * p[:, :, 1]
        return jnp.einsum('bhqk,bkhe->bqhe', a.astype(v.dtype), v)

    out = lax.map(one_block, (qb, jnp.arange(nb)))
    return jnp.moveaxis(out, 0, 1).reshape(bsz, seq, B_HEADS, B_V_DIM)


def stick_breaking_attention(q, k, v):
    bsz, seq = q.shape[:2]
    nb = seq // BLOCK
    scale = C_HEAD_DIM ** -0.5
    qb = jnp.moveaxis(q.reshape(bsz, nb, BLOCK, C_HEADS, C_HEAD_DIM), 1, 0)
    key_pos = jnp.arange(seq)

    def one_block(args):
        qblk, i = args
        q_pos = i * BLOCK + jnp.arange(BLOCK)
        strict = q_pos[:, None] > key_pos[None, :]
        z = jnp.einsum('bqhd,bkhd->bhqk', qblk, k).astype(jnp.float32) * scale
        log_beta = jax.nn.log_sigmoid(z)
        log_keep = jnp.where(strict, log_beta - z, 0.0)
        after = lax.cumsum(log_keep, axis=3, reverse=True) - log_keep
        a = jnp.where(strict, jnp.exp(log_beta + after), 0.0)
        return jnp.einsum('bhqk,bkhd->bqhd', a.astype(v.dtype), v)

    out = lax.map(one_block, (qb, jnp.arange(nb)))
    return jnp.moveaxis(out, 0, 1).reshape(bsz, seq, C_WIDTH)


def hybrid_mixer(xn, w_in, w_s, b_s, g_v, diff_lambda, subln_g, w_up_a, w_up_b, w_up_c, w_out, lambda_init):
    bsz, seq, _ = xn.shape
    proj = xn @ w_in
    points, acc = [], 0
    for width in IN_SPLITS[:-1]:
        acc += width
        points.append(acc)
    ua, va, qb, kb, vb, qc, kc, vc, gl = jnp.split(proj, points, axis=-1)

    y_a = chunked_spatial_gating(ua, va, w_s, b_s, g_v)

    lamp = diff_lambda.astype(jnp.float32)
    lam = jnp.exp(jnp.sum(lamp[0] * lamp[1])) - jnp.exp(jnp.sum(lamp[2] * lamp[3])) + lambda_init
    o_b = diff_attention(qb.reshape(bsz, seq, B_HEADS, 2, B_QK_DIM),
                         kb.reshape(bsz, seq, B_HEADS, 2, B_QK_DIM),
                         vb.reshape(bsz, seq, B_HEADS, B_V_DIM),
                         lam, alibi_slopes(B_HEADS))
    y_b = (rms_norm(o_b, subln_g) * (1.0 - lambda_init)).reshape(bsz, seq, B_WIDTH)

    y_c = stick_breaking_attention(qc.reshape(bsz, seq, C_HEADS, C_HEAD_DIM),
                                   kc.reshape(bsz, seq, C_HEADS, C_HEAD_DIM),
                                   vc.reshape(bsz, seq, C_HEADS, C_HEAD_DIM))

    g = jax.nn.sigmoid(gl.reshape(bsz, seq, N_BRANCHES, D_MODEL))
    merged = g[:, :, 0] * (y_a @ w_up_a) + g[:, :, 1] * (y_b @ w_up_b) + g[:, :, 2] * (y_c @ w_up_c)
    return merged @ w_out


def grouped_moe(xn, router_w, router_bias, w1, w3, w2):
    bsz, seq, d = xn.shape
    xt = xn.reshape(-1, d)
    aff = jax.nn.sigmoid((xt @ router_w).astype(jnp.float32))
    sel = aff + router_bias.astype(jnp.float32)
    grouped = sel.reshape(-1, N_GROUPS, EXPERTS_PER_GROUP)
    group_score = lax.top_k(grouped, TOP_K)[0].sum(-1)
    best = jnp.argmax(group_score, axis=-1)
    cand = jnp.take_along_axis(grouped, best[:, None, None], axis=1)[:, 0]
    _, local = lax.top_k(cand, TOP_K)
    idx = best[:, None] * EXPERTS_PER_GROUP + local
    w = jnp.take_along_axis(aff, idx, axis=1)
    w = w / jnp.sum(w, axis=-1, keepdims=True)
    gate = jnp.sum(jax.nn.one_hot(idx, N_EXPERTS, dtype=jnp.float32) * w[..., None], axis=1)
    h = jax.nn.silu(jnp.einsum('td,edf->tef', xt, w1)) * jnp.einsum('td,edf->tef', xt, w3)
    h = h * gate[:, :, None].astype(h.dtype)
    y = jnp.einsum('tef,efd->td', h, w2)
    return y.reshape(bsz, seq, d)


def setup_inputs(seed: int = 0) -> dict:
    key = jax.random.key(seed)
    ks = jax.random.split(key, 20)
    f32 = jnp.float32
    resid = (2 * DEPTH) ** -0.5
    nrm = lambda k, shape, s: jax.random.normal(k, shape, dtype=f32) * s
    return {
        "x": nrm(ks[0], (BATCH, SEQ, D_MODEL), 1.0),
        "norm_mix_gain": 1.0 + nrm(ks[1], (DEPTH, D_MODEL), 0.02),
        "w_in": nrm(ks[2], (DEPTH, D_MODEL, IN_WIDTH), D_MODEL ** -0.5),
        "gmlp_w_s": nrm(ks[3], (DEPTH, A_GROUPS, BLOCK, BLOCK), BLOCK ** -0.5),
        "gmlp_b_s": 1.0 + nrm(ks[4], (DEPTH, A_GROUPS, BLOCK), 0.02),
        "gmlp_v_gain": 1.0 + nrm(ks[5], (DEPTH, A_WIDTH), 0.02),
        "diff_lambda": nrm(ks[6], (DEPTH, 4, B_QK_DIM), 0.1),
        "diff_subln_gain": 1.0 + nrm(ks[7], (DEPTH, B_V_DIM), 0.02),
        "w_up_a": nrm(ks[8], (DEPTH, A_WIDTH, D_MODEL), A_WIDTH ** -0.5),
        "w_up_b": nrm(ks[9], (DEPTH, B_WIDTH, D_MODEL), B_WIDTH ** -0.5),
        "w_up_c": nrm(ks[10], (DEPTH, C_WIDTH, D_MODEL), C_WIDTH ** -0.5),
        "w_out": nrm(ks[11], (DEPTH, D_MODEL, D_MODEL), D_MODEL ** -0.5 * resid),
        "norm_ffn_gain": 1.0 + nrm(ks[12], (DEPTH, D_MODEL), 0.02),
        "router_w": nrm(ks[13], (D_MODEL, N_EXPERTS), D_MODEL ** -0.5),
        "router_bias": nrm(ks[14], (N_EXPERTS,), 0.01),
        "moe_w1": nrm(ks[15], (DEPTH, N_EXPERTS, D_MODEL, D_FF_EXPERT), D_MODEL ** -0.5),
        "moe_w3": nrm(ks[16], (DEPTH, N_EXPERTS, D_MODEL, D_FF_EXPERT), D_MODEL ** -0.5),
        "moe_w2": nrm(ks[17], (DEPTH, N_EXPERTS, D_FF_EXPERT, D_MODEL), D_FF_EXPERT ** -0.5 * resid),
        "final_gain": 1.0 + nrm(ks[18], (D_MODEL,), 0.02),
    }


def reference(x, norm_mix_gain, w_in, gmlp_w_s, gmlp_b_s, gmlp_v_gain, diff_lambda, diff_subln_gain,
              w_up_a, w_up_b, w_up_c, w_out, norm_ffn_gain, router_w, router_bias,
              moe_w1, moe_w3, moe_w2, final_gain):
    for l in range(DEPTH):
        lambda_init = 0.8 - 0.6 * math.exp(-0.3 * l)
        h = x + hybrid_mixer(rms_norm(x, norm_mix_gain[l]), w_in[l], gmlp_w_s[l], gmlp_b_s[l], gmlp_v_gain[l],
                             diff_lambda[l], diff_subln_gain[l], w_up_a[l], w_up_b[l], w_up_c[l], w_out[l],
                             lambda_init)
        x = h + grouped_moe(rms_norm(h, norm_ffn_gain[l]), router_w, router_bias,
                            moe_w1[l], moe_w3[l], moe_w2[l])
    return rms_norm(x, final_gain)
```

```python
import functools
import math

import jax
import jax.numpy as jnp
from jax import lax
from jax.experimental import pallas as pl
from jax.experimental.pallas import tpu as pltpu

D_MODEL = 1024
BLOCK = 128
A_GROUPS = 4
A_WIDTH = 512
B_HEADS = 4
B_QK_DIM = 64
B_V_DIM = 128
C_HEADS = 8
C_HEAD_DIM = 64
C_WIDTH = 512
N_EXPERTS = 16
N_GROUPS = 4
EXPERTS_PER_GROUP = 4
D_FF_EXPERT = 512
IN_WIDTH = 7168
EPS = 1e-6

LANES = 128
VMEM_LIMIT = 56 * 1024 * 1024
F32 = jnp.float32
BF16 = jnp.bfloat16
NEG_BIG = -1e30
EXP_UNDERFLOW = -104.0

COL_QB, COL_KB, COL_VB = 8, 12, 16


def _nt_dot(a, b):
    return lax.dot_general(a, b, (((1,), (1,)), ((), ())), preferred_element_type=F32)


def _dot(a, b):
    return jnp.dot(a, b, preferred_element_type=F32)


def _rms(xf, gain):
    return xf * lax.rsqrt(jnp.mean(xf * xf, axis=-1, keepdims=True) + EPS) * gain


def _in_proj_kernel(x_ref, g_ref, w_ref, o_ref, xn_ref):
    @pl.when(pl.program_id(1) == 0)
    def _():
        xn_ref[...] = _rms(x_ref[...], g_ref[...]).astype(BF16)

    o_ref[...] = _dot(xn_ref[...], w_ref[...]).astype(BF16)


def in_proj(x2, gain, w_bf16, tm=1024, tn=1024):
    t, d = x2.shape
    n = w_bf16.shape[1]
    tm = min(tm, t)
    return pl.pallas_call(
        _in_proj_kernel,
        grid=(t // tm, n // tn),
        in_specs=[
            pl.BlockSpec((tm, d), lambda i, j: (i, 0)),
            pl.BlockSpec((1, d), lambda i, j: (0, 0)),
            pl.BlockSpec((d, tn), lambda i, j: (0, j)),
        ],
        out_specs=pl.BlockSpec((tm, tn), lambda i, j: (i, j)),
        out_shape=jax.ShapeDtypeStruct((t, n), BF16),
        scratch_shapes=[pltpu.VMEM((tm, d), BF16)],
        compiler_params=pltpu.CompilerParams(
            dimension_semantics=("arbitrary", "arbitrary"), vmem_limit_bytes=VMEM_LIMIT),
        name="in_proj",
    )(x2, gain.reshape(1, d), w_bf16)


def _diff_attn_kernel(lam_ref, sg_ref, q_ref, k_ref, v_ref, o_ref, m_ref, l_ref, acc_ref,
                      *, tq, tk, lambda_init):
    h = pl.program_id(1)
    qi = pl.program_id(2)
    slope = jnp.where(h == 0, 0.25, jnp.where(h == 1, 0.0625, jnp.where(h == 2, 0.015625, 0.00390625)))
    slope = slope.astype(F32)

    lane = lax.broadcasted_iota(jnp.int32, (tq, LANES), 1)
    q = q_ref[...] * jnp.asarray(B_QK_DIM ** -0.5, BF16)
    zero = jnp.zeros_like(q)
    qs = (jnp.where(lane < B_QK_DIM, q, zero), jnp.where(lane >= B_QK_DIM, q, zero))

    m_ref[...] = jnp.full(m_ref.shape, NEG_BIG, F32)
    l_ref[...] = jnp.zeros(l_ref.shape, F32)
    acc_ref[...] = jnp.zeros(acc_ref.shape, F32)

    kcol = lax.broadcasted_iota(jnp.int32, (1, tk), 1)
    q0 = qi * tq

    def block(kb, masked):
        k0 = kb * tk
        kk = k_ref[pl.ds(pl.multiple_of(k0, tk), tk), :]
        vv = v_ref[pl.ds(pl.multiple_of(k0, tk), tk), :]
        bias = slope * (kcol + (k0 - q0)).astype(F32)
        if masked:
            row = lax.broadcasted_iota(jnp.int32, (tq, tk), 0)
            col = lax.broadcasted_iota(jnp.int32, (tq, tk), 1)
            keep = (row + q0) >= (col + k0)
        for mi in range(2):
            s = _nt_dot(qs[mi], kk) + bias
            if masked:
                s = jnp.where(keep, s, NEG_BIG)
            m_prev = m_ref[mi]
            m_new = jnp.maximum(m_prev, jnp.max(s, axis=-1, keepdims=True))
            p = jnp.exp(s - m_new)
            alpha = jnp.exp(m_prev - m_new)
            l_ref[mi] = alpha * l_ref[mi] + jnp.sum(p, axis=-1, keepdims=True)
            acc_ref[mi] = alpha * acc_ref[mi] + _dot(p.astype(BF16), vv)
            m_ref[mi] = m_new

    n_full = (qi * tq) // tk

    def body(kb, c):
        block(kb, False)
        return c

    lax.fori_loop(0, n_full, body, 0)
    for d in range(tq // tk):
        block(n_full + d, True)

    lp = lam_ref[...]
    s1 = jnp.sum(lp[0:1] * lp[1:2], axis=-1, keepdims=True)
    s2 = jnp.sum(lp[2:3] * lp[3:4], axis=-1, keepdims=True)
    lam = jnp.exp(s1) - jnp.exp(s2) + lambda_init
    o = acc_ref[0] / l_ref[0] - lam * (acc_ref[1] / l_ref[1])
    o_ref[...] = (_rms(o, sg_ref[...]) * (1.0 - lambda_init)).astype(BF16)


def diff_attn(proj, lam_params, subln_g, bsz, seq, lambda_init, tq=256, tk=256):
    tq = min(tq, seq)
    tk = min(tk, tq)
    nq = seq // tq
    kern = functools.partial(_diff_attn_kernel, tq=tq, tk=tk, lambda_init=lambda_init)
    return pl.pallas_call(
        kern,
        grid=(bsz, B_HEADS, nq),
        in_specs=[
            pl.BlockSpec((4, B_QK_DIM), lambda b, h, i: (0, 0)),
            pl.BlockSpec((1, B_V_DIM), lambda b, h, i: (0, 0)),
            pl.BlockSpec((tq, LANES), lambda b, h, i: (b * nq + i, COL_QB + h)),
            pl.BlockSpec((seq, LANES), lambda b, h, i: (b, COL_KB + h)),
            pl.BlockSpec((seq, LANES), lambda b, h, i: (b, COL_VB + h)),
        ],
        out_specs=pl.BlockSpec((tq, LANES), lambda b, h, i: (b * nq + i, h)),
        out_shape=jax.ShapeDtypeStruct((bsz * seq, B_HEADS * B_V_DIM), BF16),
        scratch_shapes=[pltpu.VMEM((2, tq, 1), F32), pltpu.VMEM((2, tq, 1), F32),
                        pltpu.VMEM((2, tq, B_V_DIM), F32)],
        compiler_params=pltpu.CompilerParams(
            dimension_semantics=("arbitrary", "arbitrary", "arbitrary"), vmem_limit_bytes=VMEM_LIMIT),
        name="diff_attn",
    )(lam_params, subln_g.reshape(1, B_V_DIM), proj, proj, proj)


def _stick_kernel(q_ref, k_ref, v_ref, o_ref, *, tq):
    tk = tq
    qi = pl.program_id(1)
    lane = lax.broadcasted_iota(jnp.int32, (tq, LANES), 1)
    low = lane < C_HEAD_DIM
    row = lax.broadcasted_iota(jnp.int32, (tq, tk), 0)
    col = lax.broadcasted_iota(jnp.int32, (tq, tk), 1)
    strict = row > col
    uj = lax.broadcasted_iota(jnp.int32, (2 * tk, tk), 0)
    us = lax.broadcasted_iota(jnp.int32, (2 * tk, tk), 1)
    u2 = jnp.where((uj % tk) > us, 1.0, 0.0).astype(BF16)

    def one_block(qm, kk, vv, carry, acc, masked):
        z = _nt_dot(qm, kk)
        sp = jnp.log1p(jnp.exp(-jnp.abs(z)))
        lb = jnp.minimum(z, 0.0) - sp
        lk = lb - z
        if masked:
            lk = jnp.where(strict, lk, 0.0)
        hi = lk.astype(BF16)
        lo = (lk - hi.astype(F32)).astype(BF16)
        after = _dot(jnp.concatenate([hi, lo], axis=1), u2) + carry
        a = jnp.exp(lb + after)
        if masked:
            a = jnp.where(strict, a, 0.0)
        acc = acc + _dot(a.astype(BF16), vv)
        carry = carry + jnp.sum(lk, axis=-1, keepdims=True)
        return carry, acc

    for p in range(C_HEADS // 2):
        cols = slice(p * LANES, (p + 1) * LANES)
        q = q_ref[:, cols] * jnp.asarray(C_HEAD_DIM ** -0.5, BF16)
        zero = jnp.zeros_like(q)
        qms = (jnp.where(low, q, zero), jnp.where(low, zero, q))

        def load(kb):
            start = pl.multiple_of(kb * tk, tk)
            return k_ref[pl.ds(start, tk), cols], v_ref[pl.ds(start, tk), cols]

        kk, vv = load(qi)
        state = []
        for a in range(2):
            state.extend(one_block(qms[a], kk, vv, jnp.zeros((tq, 1), F32),
                                   jnp.zeros((tq, LANES), F32), True))

        def cond(st):
            kb, c0, _, c1, _ = st
            alive = jnp.maximum(jnp.max(c0), jnp.max(c1)) > EXP_UNDERFLOW
            return jnp.logical_and(kb >= 0, alive)

        def body(st):
            kb, c0, a0, c1, a1 = st
            kk, vv = load(kb)
            c0, a0 = one_block(qms[0], kk, vv, c0, a0, False)
            c1, a1 = one_block(qms[1], kk, vv, c1, a1, False)
            return kb - 1, c0, a0, c1, a1

        _, _, a0, _, a1 = lax.while_loop(cond, body, (qi - 1, *state))
        o_ref[:, cols] = jnp.where(low, a0, a1).astype(BF16)


def stick_attn(proj, bsz, seq, tq=128):
    nq = seq // tq
    kern = functools.partial(_stick_kernel, tq=tq)
    return pl.pallas_call(
        kern,
        grid=(bsz, nq),
        in_specs=[
            pl.BlockSpec((tq, C_WIDTH), lambda b, i: (b * nq + i, 5)),
            pl.BlockSpec((seq, C_WIDTH), lambda b, i: (b, 6)),
            pl.BlockSpec((seq, C_WIDTH), lambda b, i: (b, 7)),
        ],
        out_specs=pl.BlockSpec((tq, C_WIDTH), lambda b, i: (b * nq + i, 0)),
        out_shape=jax.ShapeDtypeStruct((bsz * seq, C_WIDTH), BF16),
        compiler_params=pltpu.CompilerParams(
            dimension_semantics=("arbitrary", "arbitrary"), vmem_limit_bytes=VMEM_LIMIT),
        name="stick_attn",
    )(proj, proj, proj)


def _top2_route(sel, aff):
    def top2_sum(a, b, c, d):
        hi1, lo1 = jnp.maximum(a, b), jnp.minimum(a, b)
        hi2, lo2 = jnp.maximum(c, d), jnp.minimum(c, d)
        return jnp.maximum(hi1, hi2) + jnp.maximum(jnp.minimum(hi1, hi2), jnp.maximum(lo1, lo2))

    scores = [top2_sum(*sel[4 * g:4 * g + 4]) for g in range(N_GROUPS)]
    best = jnp.zeros_like(scores[0], dtype=jnp.int32)
    best_score = scores[0]
    for g in range(1, N_GROUPS):
        better = scores[g] > best_score
        best = jnp.where(better, g, best)
        best_score = jnp.where(better, scores[g], best_score)

    def pick(rows, i):
        out = rows[i]
        for g in range(1, N_GROUPS):
            out = jnp.where(best == g, rows[4 * g + i], out)
        return out

    cs = [pick(sel, i) for i in range(EXPERTS_PER_GROUP)]
    ca = [pick(aff, i) for i in range(EXPERTS_PER_GROUP)]

    def argmax_first(vals, exclude=None):
        idx = jnp.full(vals[0].shape, -1, jnp.int32)
        cur = jnp.full(vals[0].shape, -jnp.inf, F32)
        for i, v in enumerate(vals):
            ok = v > cur
            if exclude is not None:
                ok = jnp.logical_and(ok, exclude != i)
            idx = jnp.where(ok, i, idx)
            cur = jnp.where(ok, v, cur)
        return idx

    i1 = argmax_first(cs)
    i2 = argmax_first(cs, exclude=i1)

    def take(vals, idx):
        out = vals[0]
        for i in range(1, len(vals)):
            out = jnp.where(idx == i, vals[i], out)
        return out

    w1, w2 = take(ca, i1), take(ca, i2)
    tot = w1 + w2
    w1, w2 = w1 / tot, w2 / tot
    e1 = best * EXPERTS_PER_GROUP + i1
    e2 = best * EXPERTS_PER_GROUP + i2
    return [jnp.where(e1 == e, w1, 0.0) + jnp.where(e2 == e, w2, 0.0) for e in range(N_EXPERTS)]


def _merge_kernel(x_ref, u_ref, v_ref, g0_ref, g1_ref, g2_ref, yb_ref, yc_ref,
                  ws_ref, bst_ref, gv_ref, wa_ref, wb_ref, wc_ref, wo_ref, ng_ref, rwt_ref, rb_ref,
                  h_ref, xn_ref, gate_ref, ya_ref, *, tm):
    u = jax.nn.gelu(u_ref[...].astype(F32))
    v = _rms(jax.nn.gelu(v_ref[...].astype(F32)), gv_ref[...]).astype(BF16)

    r = lax.broadcasted_iota(jnp.int32, (BLOCK, BLOCK), 0)
    c = lax.broadcasted_iota(jnp.int32, (BLOCK, BLOCK), 1)
    causal = r >= c
    for g in range(A_GROUPS):
        w = jnp.where(causal, ws_ref[g], 0.0).astype(BF16)
        bias = bst_ref[:, g:g + 1]
        cols = slice(g * LANES, (g + 1) * LANES)
        for ch in range(tm // BLOCK):
            rows = slice(ch * BLOCK, (ch + 1) * BLOCK)
            mixed = _dot(w, v[rows, cols]) + bias
            ya_ref[rows, cols] = (u[rows, cols] * mixed).astype(BF16)

    merged = jax.nn.sigmoid(g0_ref[...].astype(F32)) * _dot(ya_ref[...], wa_ref[...])
    merged += jax.nn.sigmoid(g1_ref[...].astype(F32)) * _dot(yb_ref[...], wb_ref[...])
    merged += jax.nn.sigmoid(g2_ref[...].astype(F32)) * _dot(yc_ref[...], wc_ref[...])
    h = x_ref[...] + _dot(merged.astype(BF16), wo_ref[...])
    h_ref[...] = h

    xn = _rms(h, ng_ref[...])
    xn_ref[...] = xn.astype(BF16)

    xh = xn.astype(BF16)
    xl = (xn - xh.astype(F32)).astype(BF16)
    rw = rwt_ref[...]
    rh = rw.astype(BF16)
    rl = (rw - rh.astype(F32)).astype(BF16)
    logits = _nt_dot(rh, xh) + (_nt_dot(rh, xl) + _nt_dot(rl, xh))
    aff = jax.nn.sigmoid(logits)
    sel = aff + rb_ref[...]
    gates = _top2_route([sel[e:e + 1] for e in range(N_EXPERTS)],
                        [aff[e:e + 1] for e in range(N_EXPERTS)])
    gate_ref[...] = jnp.concatenate(gates, axis=0)


def merge(x2, proj, yb, yc, w_s, b_s_t, g_v, wa, wb, wc, wo, norm_g, router_wt, router_b, tm=512):
    t, d = x2.shape
    tm = min(tm, t)
    row = lambda i: (i, 0)
    const2 = lambda i: (0, 0)
    kern = functools.partial(_merge_kernel, tm=tm)
    return pl.pallas_call(
        kern,
        grid=(t // tm,),
        in_specs=[
            pl.BlockSpec((tm, d), row),
            pl.BlockSpec((tm, A_WIDTH), lambda i: (i, 0)),
            pl.BlockSpec((tm, A_WIDTH), lambda i: (i, 1)),
            pl.BlockSpec((tm, d), lambda i: (i, 4)),
            pl.BlockSpec((tm, d), lambda i: (i, 5)),
            pl.BlockSpec((tm, d), lambda i: (i, 6)),
            pl.BlockSpec((tm, A_WIDTH), row),
            pl.BlockSpec((tm, C_WIDTH), row),
            pl.BlockSpec((A_GROUPS, BLOCK, BLOCK), lambda i: (0, 0, 0)),
            pl.BlockSpec((BLOCK, A_GROUPS), const2),
            pl.BlockSpec((1, A_WIDTH), const2),
            pl.BlockSpec((A_WIDTH, d), const2),
            pl.BlockSpec((A_WIDTH, d), const2),
            pl.BlockSpec((C_WIDTH, d), const2),
            pl.BlockSpec((d, d), const2),
            pl.BlockSpec((1, d), const2),
            pl.BlockSpec((N_EXPERTS, d), const2),
            pl.BlockSpec((N_EXPERTS, 1), const2),
        ],
        out_specs=[
            pl.BlockSpec((tm, d), row),
            pl.BlockSpec((tm, d), row),
            pl.BlockSpec((N_EXPERTS, tm), lambda i: (0, i)),
        ],
        out_shape=[
            jax.ShapeDtypeStruct((t, d), F32),
            jax.ShapeDtypeStruct((t, d), BF16),
            jax.ShapeDtypeStruct((N_EXPERTS, t), F32),
        ],
        scratch_shapes=[pltpu.VMEM((tm, A_WIDTH), BF16)],
        compiler_params=pltpu.CompilerParams(
            dimension_semantics=("arbitrary",), vmem_limit_bytes=VMEM_LIMIT),
        name="merge",
    )(x2, proj, proj, proj, proj, proj, yb, yc, w_s, b_s_t, g_v.reshape(1, A_WIDTH),
      wa, wb, wc, wo, norm_g.reshape(1, d), router_wt, router_b.reshape(N_EXPERTS, 1))


def _moe_kernel(h_ref, xn_ref, gate_ref, w1_ref, w3_ref, w2_ref, fg_ref, o_ref, acc_ref, *, final_norm):
    e = pl.program_id(1)

    @pl.when(e == 0)
    def _():
        acc_ref[...] = jnp.zeros(acc_ref.shape, F32)

    lane = lax.broadcasted_iota(jnp.int32, gate_ref.shape, 1)
    gcol = jnp.sum(jnp.where(lane == e, gate_ref[...], 0.0), axis=-1, keepdims=True)
    x = xn_ref[...]
    a = _dot(x, w1_ref[...])
    b = _dot(x, w3_ref[...])
    hh = (jax.nn.silu(a) * b) * gcol
    acc_ref[...] += _dot(hh.astype(BF16), w2_ref[...])

    @pl.when(e == N_EXPERTS - 1)
    def _():
        y = h_ref[...] + acc_ref[...]
        if final_norm:
            y = _rms(y, fg_ref[...])
        o_ref[...] = y


def moe(h, xn, gate, w1, w3, w2, final_gain, final_norm, tm=1024):
    t, d = h.shape
    tm = min(tm, t)
    kern = functools.partial(_moe_kernel, final_norm=final_norm)
    return pl.pallas_call(
        kern,
        grid=(t // tm, N_EXPERTS),
        in_specs=[
            pl.BlockSpec((tm, d), lambda i, e: (i, 0)),
            pl.BlockSpec((tm, d), lambda i, e: (i, 0)),
            pl.BlockSpec((tm, N_EXPERTS), lambda i, e: (i, 0)),
            pl.BlockSpec((None, d, D_FF_EXPERT), lambda i, e: (e, 0, 0)),
            pl.BlockSpec((None, d, D_FF_EXPERT), lambda i, e: (e, 0, 0)),
            pl.BlockSpec((None, D_FF_EXPERT, d), lambda i, e: (e, 0, 0)),
            pl.BlockSpec((1, d), lambda i, e: (0, 0)),
        ],
        out_specs=pl.BlockSpec((tm, d), lambda i, e: (i, 0)),
        out_shape=jax.ShapeDtypeStruct((t, d), F32),
        scratch_shapes=[pltpu.VMEM((tm, d), F32)],
        compiler_params=pltpu.CompilerParams(
            dimension_semantics=("arbitrary", "arbitrary"), vmem_limit_bytes=VMEM_LIMIT),
        name="moe",
    )(h, xn, gate, w1, w3, w2, final_gain.reshape(1, d))


def kernel(x, norm_mix_gain, w_in, gmlp_w_s, gmlp_b_s, gmlp_v_gain, diff_lambda, diff_subln_gain,
           w_up_a, w_up_b, w_up_c, w_out, norm_ffn_gain, router_w, router_bias,
           moe_w1, moe_w3, moe_w2, final_gain):
    bsz, seq, d = x.shape
    depth = w_in.shape[0]
    x2 = x.reshape(bsz * seq, d)
    router_wt = router_w.T
    for l in range(depth):
        lambda_init = 0.8 - 0.6 * math.exp(-0.3 * l)
        proj = in_proj(x2, norm_mix_gain[l], w_in[l].astype(BF16))
        yb = diff_attn(proj, diff_lambda[l], diff_subln_gain[l], bsz, seq, lambda_init)
        yc = stick_attn(proj, bsz, seq)
        h, xn, gate_t = merge(x2, proj, yb, yc, gmlp_w_s[l], gmlp_b_s[l].T, gmlp_v_gain[l],
                              w_up_a[l].astype(BF16), w_up_b[l].astype(BF16), w_up_c[l].astype(BF16),
                              w_out[l].astype(BF16), norm_ffn_gain[l], router_wt, router_bias)
        x2 = moe(h, xn, gate_t.T, moe_w1[l].astype(BF16), moe_w3[l].astype(BF16),
                 moe_w2[l].astype(BF16), final_gain, final_norm=(l == depth - 1))
    return x2.reshape(bsz, seq, d)
```

```python
import functools
import math

import jax
import jax.numpy as jnp
from jax import lax
from jax.experimental import pallas as pl
from jax.experimental.pallas import tpu as pltpu

D_MODEL = 1024
BLOCK = 128
A_GROUPS = 4
A_WIDTH = 512
B_HEADS = 4
B_QK_DIM = 64
B_V_DIM = 128
C_HEADS = 8
C_HEAD_DIM = 64
C_WIDTH = 512
N_EXPERTS = 16
N_GROUPS = 4
EXPERTS_PER_GROUP = 4
D_FF_EXPERT = 512
IN_WIDTH = 7168
EPS = 1e-6

LANES = 128
VMEM_LIMIT = 56 * 1024 * 1024
F32 = jnp.float32
BF16 = jnp.bfloat16
NEG_BIG = -1e30
EXP_UNDERFLOW = -104.0

COL_QB, COL_KB, COL_VB = 8, 12, 16


def _nt_dot(a, b):
    return lax.dot_general(a, b, (((1,), (1,)), ((), ())), preferred_element_type=F32)


def _dot(a, b):
    return jnp.dot(a, b, preferred_element_type=F32)


def _rms(xf, gain):
    return xf * lax.rsqrt(jnp.mean(xf * xf, axis=-1, keepdims=True) + EPS) * gain


def _in_proj_kernel(x_ref, g_ref, w_ref, o_ref, xn_ref):
    @pl.when(pl.program_id(1) == 0)
    def _():
        xn_ref[...] = _rms(x_ref[...], g_ref[...]).astype(BF16)

    o_ref[...] = _dot(xn_ref[...], w_ref[...]).astype(BF16)


def in_proj(x2, gain, w_bf16, tm=1024, tn=1024):
    t, d = x2.shape
    n = w_bf16.shape[1]
    tm = min(tm, t)
    return pl.pallas_call(
        _in_proj_kernel,
        grid=(t // tm, n // tn),
        in_specs=[
            pl.BlockSpec((tm, d), lambda i, j: (i, 0)),
            pl.BlockSpec((1, d), lambda i, j: (0, 0)),
            pl.BlockSpec((d, tn), lambda i, j: (0, j)),
        ],
        out_specs=pl.BlockSpec((tm, tn), lambda i, j: (i, j)),
        out_shape=jax.ShapeDtypeStruct((t, n), BF16),
        scratch_shapes=[pltpu.VMEM((tm, d), BF16)],
        compiler_params=pltpu.CompilerParams(
            dimension_semantics=("arbitrary", "arbitrary"), vmem_limit_bytes=VMEM_LIMIT),
        name="in_proj",
    )(x2, gain.reshape(1, d), w_bf16)


def _diff_attn_kernel(lam_ref, sg_ref, q_ref, k_ref, v_ref, o_ref, m_ref, l_ref, acc_ref,
                      *, tq, tk, lambda_init):
    h = pl.program_id(1)
    qi = pl.program_id(2)
    nc = tk // LANES
    slope = jnp.where(h == 0, 0.25, jnp.where(h == 1, 0.0625, jnp.where(h == 2, 0.015625, 0.00390625)))
    slope = slope.astype(F32)

    lane = lax.broadcasted_iota(jnp.int32, (tq, LANES), 1)
    q = q_ref[...] * jnp.asarray(B_QK_DIM ** -0.5, BF16)
    zero = jnp.zeros_like(q)
    qs = (jnp.where(lane < B_QK_DIM, q, zero), jnp.where(lane >= B_QK_DIM, q, zero))

    m_ref[...] = jnp.full(m_ref.shape, NEG_BIG, F32)
    l_ref[...] = jnp.zeros(l_ref.shape, F32)
    acc_ref[...] = jnp.zeros(acc_ref.shape, F32)

    kcol = lax.broadcasted_iota(jnp.int32, (1, LANES), 1)
    row_minus_col = lax.broadcasted_iota(jnp.int32, (tq, LANES), 0) - lane
    q0 = qi * tq

    def block(kb, masked):
        k0 = kb * tk
        kk = k_ref[pl.ds(pl.multiple_of(k0, tk), tk), :]
        vv = v_ref[pl.ds(pl.multiple_of(k0, tk), tk), :]
        biases = [slope * (kcol + (k0 - q0 + c * LANES)).astype(F32) for c in range(nc)]
        ss = [_nt_dot(qs[mi], kk) for mi in range(2)]
        for mi in range(2):
            s = ss[mi]
            cols = []
            for c in range(nc):
                sc = s[:, c * LANES:(c + 1) * LANES] + biases[c]
                if masked:
                    sc = jnp.where(row_minus_col >= (k0 - q0 + c * LANES), sc, NEG_BIG)
                cols.append(sc)
            mx = cols[0]
            for sc in cols[1:]:
                mx = jnp.maximum(mx, sc)
            m_prev = m_ref[mi]
            m_new = jnp.maximum(m_prev, jnp.max(mx, axis=-1, keepdims=True))
            alpha = jnp.exp(m_prev - m_new)
            ps = [jnp.exp(sc - m_new) for sc in cols]
            psum = ps[0]
            for pc in ps[1:]:
                psum = psum + pc
            l_ref[mi] = alpha * l_ref[mi] + psum
            p = jnp.concatenate([pc.astype(BF16) for pc in ps], axis=1)
            acc_ref[mi] = alpha * acc_ref[mi] + _dot(p, vv)
            m_ref[mi] = m_new

    n_full = (qi * tq) // tk

    def body(kb, c):
        block(kb, False)
        return c

    lax.fori_loop(0, n_full, body, 0)
    for d in range(tq // tk):
        block(n_full + d, True)

    lp = lam_ref[...]
    s1 = jnp.sum(lp[0:1] * lp[1:2], axis=-1, keepdims=True)
    s2 = jnp.sum(lp[2:3] * lp[3:4], axis=-1, keepdims=True)
    lam = jnp.exp(s1) - jnp.exp(s2) + lambda_init
    l0 = jnp.sum(l_ref[0], axis=-1, keepdims=True)
    l1 = jnp.sum(l_ref[1], axis=-1, keepdims=True)
    o = acc_ref[0] / l0 - lam * (acc_ref[1] / l1)
    o_ref[...] = (_rms(o, sg_ref[...]) * (1.0 - lambda_init)).astype(BF16)


def diff_attn(proj, lam_params, subln_g, bsz, seq, lambda_init, tq=512, tk=512):
    tq = min(tq, seq)
    tk = min(tk, tq)
    nq = seq // tq
    kern = functools.partial(_diff_attn_kernel, tq=tq, tk=tk, lambda_init=lambda_init)
    return pl.pallas_call(
        kern,
        grid=(bsz, B_HEADS, nq),
        in_specs=[
            pl.BlockSpec((4, B_QK_DIM), lambda b, h, i: (0, 0)),
            pl.BlockSpec((1, B_V_DIM), lambda b, h, i: (0, 0)),
            pl.BlockSpec((tq, LANES), lambda b, h, i: (b * nq + i, COL_QB + h)),
            pl.BlockSpec((seq, LANES), lambda b, h, i: (b, COL_KB + h)),
            pl.BlockSpec((seq, LANES), lambda b, h, i: (b, COL_VB + h)),
        ],
        out_specs=pl.BlockSpec((tq, LANES), lambda b, h, i: (b * nq + i, h)),
        out_shape=jax.ShapeDtypeStruct((bsz * seq, B_HEADS * B_V_DIM), BF16),
        scratch_shapes=[pltpu.VMEM((2, tq, LANES), F32), pltpu.VMEM((2, tq, LANES), F32),
                        pltpu.VMEM((2, tq, B_V_DIM), F32)],
        compiler_params=pltpu.CompilerParams(
            dimension_semantics=("arbitrary", "arbitrary", "arbitrary"), vmem_limit_bytes=VMEM_LIMIT),
        name="diff_attn",
    )(lam_params, subln_g.reshape(1, B_V_DIM), proj, proj, proj)


def _stick_kernel(q_ref, k_ref, v_ref, o_ref, qs_ref, carry_ref, acc_ref, *, tq):
    tk = tq
    n_pairs = C_HEADS // 2
    qi = pl.program_id(1)
    lane = lax.broadcasted_iota(jnp.int32, (tq, LANES), 1)
    low = lane < C_HEAD_DIM
    for p in range(n_pairs):
        q = q_ref[:, p * LANES:(p + 1) * LANES] * jnp.asarray(C_HEAD_DIM ** -0.5, BF16)
        zero = jnp.zeros_like(q)
        qs_ref[p, 0:tq] = jnp.where(low, q, zero)
        qs_ref[p, tq:2 * tq] = jnp.where(low, zero, q)
    carry_ref[...] = jnp.zeros(carry_ref.shape, F32)
    acc_ref[...] = jnp.zeros(acc_ref.shape, F32)

    row = lax.broadcasted_iota(jnp.int32, (2 * tq, tk), 0)
    col = lax.broadcasted_iota(jnp.int32, (2 * tq, tk), 1)
    strict = (row % tq) > col
    uj = lax.broadcasted_iota(jnp.int32, (2 * tk, tk + LANES), 0) % tk
    us = lax.broadcasted_iota(jnp.int32, (2 * tk, tk + LANES), 1)
    w2 = jnp.where(jnp.logical_or(uj > us, us >= tk), 1.0, 0.0).astype(BF16)
    vlow = lax.broadcasted_iota(jnp.int32, (tk, LANES), 1) < C_HEAD_DIM

    def step(kb, masked):
        start = pl.multiple_of(kb * tk, tk)
        pairs = range(n_pairs)
        cols = [slice(p * LANES, (p + 1) * LANES) for p in pairs]
        zs = [_nt_dot(qs_ref[p], k_ref[pl.ds(start, tk), cols[p]]) for p in pairs]
        lbs, rs = [], []
        for p in pairs:
            z = zs[p]
            sp = jnp.log(1.0 + jnp.exp(-jnp.abs(z)))
            lb = jnp.minimum(z, 0.0) - sp
            lk = lb - z
            if masked:
                lk = jnp.where(strict, lk, 0.0)
            hi = lk.astype(BF16)
            lo = (lk - hi.astype(F32)).astype(BF16)
            lbs.append(lb)
            rs.append(_dot(jnp.concatenate([hi, lo], axis=1), w2))
        mx = None
        for p in pairs:
            carry = carry_ref[p]
            a = jnp.exp(lbs[p] + rs[p][:, :tk] + carry)
            if masked:
                a = jnp.where(strict, a, 0.0)
            carry = carry + rs[p][:, tk:]
            carry_ref[p] = carry
            mx = carry if mx is None else jnp.maximum(mx, carry)
            ab = a.astype(BF16)
            vv = v_ref[pl.ds(start, tk), cols[p]]
            zero = jnp.zeros_like(vv)
            vst = jnp.concatenate([jnp.where(vlow, vv, zero), jnp.where(vlow, zero, vv)], axis=0)
            acc_ref[p] += _dot(jnp.concatenate([ab[:tq], ab[tq:]], axis=1), vst)
        return jnp.max(mx)

    def cond(st):
        kb, alive = st
        return jnp.logical_and(kb >= 0, alive > EXP_UNDERFLOW)

    def body(st):
        kb, _ = st
        return kb - 1, step(kb, False)

    lax.while_loop(cond, body, (qi - 1, step(qi, True)))
    for p in range(n_pairs):
        o_ref[:, p * LANES:(p + 1) * LANES] = acc_ref[p].astype(BF16)


def stick_attn(proj, bsz, seq, tq=128):
    nq = seq // tq
    kern = functools.partial(_stick_kernel, tq=tq)
    n_pairs = C_HEADS // 2
    return pl.pallas_call(
        kern,
        grid=(bsz, nq),
        in_specs=[
            pl.BlockSpec((tq, C_WIDTH), lambda b, i: (b * nq + i, 5)),
            pl.BlockSpec((seq, C_WIDTH), lambda b, i: (b, 6)),
            pl.BlockSpec((seq, C_WIDTH), lambda b, i: (b, 7)),
        ],
        out_specs=pl.BlockSpec((tq, C_WIDTH), lambda b, i: (b * nq + i, 0)),
        out_shape=jax.ShapeDtypeStruct((bsz * seq, C_WIDTH), BF16),
        scratch_shapes=[pltpu.VMEM((n_pairs, 2 * tq, LANES), BF16),
                        pltpu.VMEM((n_pairs, 2 * tq, LANES), F32),
                        pltpu.VMEM((n_pairs, tq, LANES), F32)],
        compiler_params=pltpu.CompilerParams(
            dimension_semantics=("arbitrary", "arbitrary"), vmem_limit_bytes=VMEM_LIMIT),
        name="stick_attn",
    )(proj, proj, proj)


def _top2_route(sel, aff):
    def top2_sum(a, b, c, d):
        hi1, lo1 = jnp.maximum(a, b), jnp.minimum(a, b)
        hi2, lo2 = jnp.maximum(c, d), jnp.minimum(c, d)
        return jnp.maximum(hi1, hi2) + jnp.maximum(jnp.minimum(hi1, hi2), jnp.maximum(lo1, lo2))

    scores = [top2_sum(*sel[4 * g:4 * g + 4]) for g in range(N_GROUPS)]
    best = jnp.zeros_like(scores[0], dtype=jnp.int32)
    best_score = scores[0]
    for g in range(1, N_GROUPS):
        better = scores[g] > best_score
        best = jnp.where(better, g, best)
        best_score = jnp.where(better, scores[g], best_score)

    def pick(rows, i):
        out = rows[i]
        for g in range(1, N_GROUPS):
            out = jnp.where(best == g, rows[4 * g + i], out)
        return out

    cs = [pick(sel, i) for i in range(EXPERTS_PER_GROUP)]
    ca = [pick(aff, i) for i in range(EXPERTS_PER_GROUP)]

    def argmax_first(vals, exclude=None):
        idx = jnp.full(vals[0].shape, -1, jnp.int32)
        cur = jnp.full(vals[0].shape, -jnp.inf, F32)
        for i, v in enumerate(vals):
            ok = v > cur
            if exclude is not None:
                ok = jnp.logical_and(ok, exclude != i)
            idx = jnp.where(ok, i, idx)
            cur = jnp.where(ok, v, cur)
        return idx

    i1 = argmax_first(cs)
    i2 = argmax_first(cs, exclude=i1)

    def take(vals, idx):
        out = vals[0]
        for i in range(1, len(vals)):
            out = jnp.where(idx == i, vals[i], out)
        return out

    w1, w2 = take(ca, i1), take(ca, i2)
    tot = w1 + w2
    w1, w2 = w1 / tot, w2 / tot
    e1 = best * EXPERTS_PER_GROUP + i1
    e2 = best * EXPERTS_PER_GROUP + i2
    return [jnp.where(e1 == e, w1, 0.0) + jnp.where(e2 == e, w2, 0.0) for e in range(N_EXPERTS)]


def _merge_kernel(x_ref, u_ref, v_ref, g0_ref, g1_ref, g2_ref, yb_ref, yc_ref,
                  ws_ref, bst_ref, gv_ref, wa_ref, wb_ref, wc_ref, wo_ref, ng_ref, rwt_ref, rb_ref,
                  h_ref, xn_ref, gate_ref, ya_ref, *, tm):
    u = jax.nn.gelu(u_ref[...].astype(F32))
    v = _rms(jax.nn.gelu(v_ref[...].astype(F32)), gv_ref[...]).astype(BF16)

    r = lax.broadcasted_iota(jnp.int32, (BLOCK, BLOCK), 0)
    c = lax.broadcasted_iota(jnp.int32, (BLOCK, BLOCK), 1)
    causal = r >= c
    for g in range(A_GROUPS):
        w = jnp.where(causal, ws_ref[g], 0.0).astype(BF16)
        bias = bst_ref[:, g:g + 1]
        cols = slice(g * LANES, (g + 1) * LANES)
        for ch in range(tm // BLOCK):
            rows = slice(ch * BLOCK, (ch + 1) * BLOCK)
            mixed = _dot(w, v[rows, cols]) + bias
            ya_ref[rows, cols] = (u[rows, cols] * mixed).astype(BF16)

    merged = jax.nn.sigmoid(g0_ref[...].astype(F32)) * _dot(ya_ref[...], wa_ref[...])
    merged += jax.nn.sigmoid(g1_ref[...].astype(F32)) * _dot(yb_ref[...], wb_ref[...])
    merged += jax.nn.sigmoid(g2_ref[...].astype(F32)) * _dot(yc_ref[...], wc_ref[...])
    h = x_ref[...] + _dot(merged.astype(BF16), wo_ref[...])
    h_ref[...] = h

    xn = _rms(h, ng_ref[...])
    xn_ref[...] = xn.astype(BF16)

    xh = xn.astype(BF16)
    xl = (xn - xh.astype(F32)).astype(BF16)
    rw = rwt_ref[...]
    rh = rw.astype(BF16)
    rl = (rw - rh.astype(F32)).astype(BF16)
    logits = _nt_dot(rh, xh) + (_nt_dot(rh, xl) + _nt_dot(rl, xh))
    aff = jax.nn.sigmoid(logits)
    sel = aff + rb_ref[...]
    gates = _top2_route([sel[e:e + 1] for e in range(N_EXPERTS)],
                        [aff[e:e + 1] for e in range(N_EXPERTS)])
    gate_ref[...] = jnp.concatenate(gates, axis=0)


def merge(x2, proj, yb, yc, w_s, b_s_t, g_v, wa, wb, wc, wo, norm_g, router_wt, router_b, tm=512):
    t, d = x2.shape
    tm = min(tm, t)
    row = lambda i: (i, 0)
    const2 = lambda i: (0, 0)
    kern = functools.partial(_merge_kernel, tm=tm)
    return pl.pallas_call(
        kern,
        grid=(t // tm,),
        in_specs=[
            pl.BlockSpec((tm, d), row),
            pl.BlockSpec((tm, A_WIDTH), lambda i: (i, 0)),
            pl.BlockSpec((tm, A_WIDTH), lambda i: (i, 1)),
            pl.BlockSpec((tm, d), lambda i: (i, 4)),
            pl.BlockSpec((tm, d), lambda i: (i, 5)),
            pl.BlockSpec((tm, d), lambda i: (i, 6)),
            pl.BlockSpec((tm, A_WIDTH), row),
            pl.BlockSpec((tm, C_WIDTH), row),
            pl.BlockSpec((A_GROUPS, BLOCK, BLOCK), lambda i: (0, 0, 0)),
            pl.BlockSpec((BLOCK, A_GROUPS), const2),
            pl.BlockSpec((1, A_WIDTH), const2),
            pl.BlockSpec((A_WIDTH, d), const2),
            pl.BlockSpec((A_WIDTH, d), const2),
            pl.BlockSpec((C_WIDTH, d), const2),
            pl.BlockSpec((d, d), const2),
            pl.BlockSpec((1, d), const2),
            pl.BlockSpec((N_EXPERTS, d), const2),
            pl.BlockSpec((N_EXPERTS, 1), const2),
        ],
        out_specs=[
            pl.BlockSpec((tm, d), row),
            pl.BlockSpec((tm, d), row),
            pl.BlockSpec((N_EXPERTS, tm), lambda i: (0, i)),
        ],
        out_shape=[
            jax.ShapeDtypeStruct((t, d), F32),
            jax.ShapeDtypeStruct((t, d), BF16),
            jax.ShapeDtypeStruct((N_EXPERTS, t), F32),
        ],
        scratch_shapes=[pltpu.VMEM((tm, A_WIDTH), BF16)],
        compiler_params=pltpu.CompilerParams(
            dimension_semantics=("arbitrary",), vmem_limit_bytes=VMEM_LIMIT),
        name="merge",
    )(x2, proj, proj, proj, proj, proj, yb, yc, w_s, b_s_t, g_v.reshape(1, A_WIDTH),
      wa, wb, wc, wo, norm_g.reshape(1, d), router_wt, router_b.reshape(N_EXPERTS, 1))


def _moe_kernel(h_ref, xn_ref, gate_ref, w1_ref, w3_ref, w2_ref, fg_ref, o_ref, acc_ref, *, final_norm):
    e = pl.program_id(1)

    @pl.when(e == 0)
    def _():
        acc_ref[...] = jnp.zeros(acc_ref.shape, F32)

    lane = lax.broadcasted_iota(jnp.int32, gate_ref.shape, 1)
    gcol = jnp.sum(jnp.where(lane == e, gate_ref[...], 0.0), axis=-1, keepdims=True)
    x = xn_ref[...]
    a = _dot(x, w1_ref[...])
    b = _dot(x, w3_ref[...])
    hh = (jax.nn.silu(a) * b) * gcol
    acc_ref[...] += _dot(hh.astype(BF16), w2_ref[...])

    @pl.when(e == N_EXPERTS - 1)
    def _():
        y = h_ref[...] + acc_ref[...]
        if final_norm:
            y = _rms(y, fg_ref[...])
        o_ref[...] = y


def moe(h, xn, gate, w1, w3, w2, final_gain, final_norm, tm=1024):
    t, d = h.shape
    tm = min(tm, t)
    kern = functools.partial(_moe_kernel, final_norm=final_norm)
    return pl.pallas_call(
        kern,
        grid=(t // tm, N_EXPERTS),
        in_specs=[
            pl.BlockSpec((tm, d), lambda i, e: (i, 0)),
            pl.BlockSpec((tm, d), lambda i, e: (i, 0)),
            pl.BlockSpec((tm, N_EXPERTS), lambda i, e: (i, 0)),
            pl.BlockSpec((None, d, D_FF_EXPERT), lambda i, e: (e, 0, 0)),
            pl.BlockSpec((None, d, D_FF_EXPERT), lambda i, e: (e, 0, 0)),
            pl.BlockSpec((None, D_FF_EXPERT, d), lambda i, e: (e, 0, 0)),
            pl.BlockSpec((1, d), lambda i, e: (0, 0)),
        ],
        out_specs=pl.BlockSpec((tm, d), lambda i, e: (i, 0)),
        out_shape=jax.ShapeDtypeStruct((t, d), F32),
        scratch_shapes=[pltpu.VMEM((tm, d), F32)],
        compiler_params=pltpu.CompilerParams(
            dimension_semantics=("arbitrary", "arbitrary"), vmem_limit_bytes=VMEM_LIMIT),
        name="moe",
    )(h, xn, gate, w1, w3, w2, final_gain.reshape(1, d))


def kernel(x, norm_mix_gain, w_in, gmlp_w_s, gmlp_b_s, gmlp_v_gain, diff_lambda, diff_subln_gain,
           w_up_a, w_up_b, w_up_c, w_out, norm_ffn_gain, router_w, router_bias,
           moe_w1, moe_w3, moe_w2, final_gain):
    bsz, seq, d = x.shape
    depth = w_in.shape[0]
    x2 = x.reshape(bsz * seq, d)
    router_wt = router_w.T
    for l in range(depth):
        lambda_init = 0.8 - 0.6 * math.exp(-0.3 * l)
        proj = in_proj(x2, norm_mix_gain[l], w_in[l].astype(BF16))
        yb = diff_attn(proj, diff_lambda[l], diff_subln_gain[l], bsz, seq, lambda_init)
        yc = stick_attn(proj, bsz, seq)
        h, xn, gate_t = merge(x2, proj, yb, yc, gmlp_w_s[l], gmlp_b_s[l].T, gmlp_v_gain[l],
                              w_up_a[l].astype(BF16), w_up_b[l].astype(BF16), w_up_c[l].astype(BF16),
                              w_out[l].astype(BF16), norm_ffn_gain[l], router_wt, router_bias)
        x2 = moe(h, xn, gate_t.T, moe_w1[l].astype(BF16), moe_w3[l].astype(BF16),
                 moe_w2[l].astype(BF16), final_gain, final_norm=(l == depth - 1))
    return x2.reshape(bsz, seq, d)
```

```python
import functools
import math

import jax
import jax.numpy as jnp
from jax import lax
from jax.experimental import pallas as pl
from jax.experimental.pallas import tpu as pltpu

D_MODEL = 1024
BLOCK = 128
A_GROUPS = 4
A_WIDTH = 512
B_HEADS = 4
B_QK_DIM = 64
B_V_DIM = 128
C_HEADS = 8
C_HEAD_DIM = 64
C_WIDTH = 512
N_EXPERTS = 16
N_GROUPS = 4
EXPERTS_PER_GROUP = 4
D_FF_EXPERT = 512
IN_WIDTH = 7168
XE_WIDTH = D_MODEL + 128
EPS = 1e-6

LANES = 128
VMEM_LIMIT = 56 * 1024 * 1024
F32 = jnp.float32
BF16 = jnp.bfloat16
NEG_BIG = -1e30
EXP_UNDERFLOW = -104.0

COL_QB, COL_KB, COL_VB = 8, 12, 16


def _nt_dot(a, b):
    return lax.dot_general(a, b, (((1,), (1,)), ((), ())), preferred_element_type=F32)


def _dot(a, b):
    return jnp.dot(a, b, preferred_element_type=F32)


def _rms(xf, gain):
    return xf * lax.rsqrt(jnp.mean(xf * xf, axis=-1, keepdims=True) + EPS) * gain


def _in_proj_kernel(x_ref, g_ref, w_ref, o_ref, xn_ref):
    @pl.when(pl.program_id(1) == 0)
    def _():
        xn_ref[...] = _rms(x_ref[...], g_ref[...]).astype(BF16)

    o_ref[...] = _dot(xn_ref[...], w_ref[...]).astype(BF16)


def in_proj(x2, gain, w_bf16, tm=1024, tn=1024):
    t, d = x2.shape
    n = w_bf16.shape[1]
    tm = min(tm, t)
    return pl.pallas_call(
        _in_proj_kernel,
        grid=(t // tm, n // tn),
        in_specs=[
            pl.BlockSpec((tm, d), lambda i, j: (i, 0)),
            pl.BlockSpec((1, d), lambda i, j: (0, 0)),
            pl.BlockSpec((d, tn), lambda i, j: (0, j)),
        ],
        out_specs=pl.BlockSpec((tm, tn), lambda i, j: (i, j)),
        out_shape=jax.ShapeDtypeStruct((t, n), BF16),
        scratch_shapes=[pltpu.VMEM((tm, d), BF16)],
        compiler_params=pltpu.CompilerParams(
            dimension_semantics=("arbitrary", "arbitrary"), vmem_limit_bytes=VMEM_LIMIT),
        name="in_proj",
    )(x2, gain.reshape(1, d), w_bf16)


def _diff_attn_kernel(lam_ref, sg_ref, q_ref, k_ref, v_ref, o_ref, m_ref, l_ref, acc_ref,
                      *, tq, tk, lambda_init):
    h = pl.program_id(1)
    qi = pl.program_id(2)
    nc = tk // LANES
    slope = jnp.where(h == 0, 0.25, jnp.where(h == 1, 0.0625, jnp.where(h == 2, 0.015625, 0.00390625)))
    slope = slope.astype(F32)

    lane = lax.broadcasted_iota(jnp.int32, (tq, LANES), 1)
    q = q_ref[...] * jnp.asarray(B_QK_DIM ** -0.5, BF16)
    zero = jnp.zeros_like(q)
    qs = (jnp.where(lane < B_QK_DIM, q, zero), jnp.where(lane >= B_QK_DIM, q, zero))

    m_ref[...] = jnp.full(m_ref.shape, NEG_BIG, F32)
    l_ref[...] = jnp.zeros(l_ref.shape, F32)
    acc_ref[...] = jnp.zeros(acc_ref.shape, F32)

    kcol = lax.broadcasted_iota(jnp.int32, (1, LANES), 1)
    row_minus_col = lax.broadcasted_iota(jnp.int32, (tq, LANES), 0) - lane
    q0 = qi * tq

    def block(kb, masked):
        k0 = kb * tk
        kk = k_ref[pl.ds(pl.multiple_of(k0, tk), tk), :]
        vv = v_ref[pl.ds(pl.multiple_of(k0, tk), tk), :]
        biases = [slope * (kcol + (k0 - q0 + c * LANES)).astype(F32) for c in range(nc)]
        ss = [_nt_dot(qs[mi], kk) for mi in range(2)]
        for mi in range(2):
            s = ss[mi]
            cols = []
            for c in range(nc):
                sc = s[:, c * LANES:(c + 1) * LANES] + biases[c]
                if masked:
                    sc = jnp.where(row_minus_col >= (k0 - q0 + c * LANES), sc, NEG_BIG)
                cols.append(sc)
            mx = cols[0]
            for sc in cols[1:]:
                mx = jnp.maximum(mx, sc)
            m_prev = m_ref[mi]
            m_new = jnp.maximum(m_prev, jnp.max(mx, axis=-1, keepdims=True))
            alpha = jnp.exp(m_prev - m_new)
            ps = [jnp.exp(sc - m_new) for sc in cols]
            psum = ps[0]
            for pc in ps[1:]:
                psum = psum + pc
            l_ref[mi] = alpha * l_ref[mi] + psum
            p = jnp.concatenate([pc.astype(BF16) for pc in ps], axis=1)
            acc_ref[mi] = alpha * acc_ref[mi] + _dot(p, vv)
            m_ref[mi] = m_new

    n_full = (qi * tq) // tk

    def body(kb, c):
        block(kb, False)
        return c

    lax.fori_loop(0, n_full, body, 0)
    for d in range(tq // tk):
        block(n_full + d, True)

    lp = lam_ref[...]
    s1 = jnp.sum(lp[0:1] * lp[1:2], axis=-1, keepdims=True)
    s2 = jnp.sum(lp[2:3] * lp[3:4], axis=-1, keepdims=True)
    lam = jnp.exp(s1) - jnp.exp(s2) + lambda_init
    l0 = jnp.sum(l_ref[0], axis=-1, keepdims=True)
    l1 = jnp.sum(l_ref[1], axis=-1, keepdims=True)
    o = acc_ref[0] / l0 - lam * (acc_ref[1] / l1)
    o_ref[...] = (_rms(o, sg_ref[...]) * (1.0 - lambda_init)).astype(BF16)


def diff_attn(proj, lam_params, subln_g, bsz, seq, lambda_init, tq=512, tk=512):
    tq = min(tq, seq)
    tk = min(tk, tq)
    nq = seq // tq
    kern = functools.partial(_diff_attn_kernel, tq=tq, tk=tk, lambda_init=lambda_init)
    return pl.pallas_call(
        kern,
        grid=(bsz, B_HEADS, nq),
        in_specs=[
            pl.BlockSpec((4, B_QK_DIM), lambda b, h, i: (0, 0)),
            pl.BlockSpec((1, B_V_DIM), lambda b, h, i: (0, 0)),
            pl.BlockSpec((tq, LANES), lambda b, h, i: (b * nq + i, COL_QB + h)),
            pl.BlockSpec((seq, LANES), lambda b, h, i: (b, COL_KB + h)),
            pl.BlockSpec((seq, LANES), lambda b, h, i: (b, COL_VB + h)),
        ],
        out_specs=pl.BlockSpec((tq, LANES), lambda b, h, i: (b * nq + i, h)),
        out_shape=jax.ShapeDtypeStruct((bsz * seq, B_HEADS * B_V_DIM), BF16),
        scratch_shapes=[pltpu.VMEM((2, tq, LANES), F32), pltpu.VMEM((2, tq, LANES), F32),
                        pltpu.VMEM((2, tq, B_V_DIM), F32)],
        compiler_params=pltpu.CompilerParams(
            dimension_semantics=("arbitrary", "arbitrary", "arbitrary"), vmem_limit_bytes=VMEM_LIMIT),
        name="diff_attn",
    )(lam_params, subln_g.reshape(1, B_V_DIM), proj, proj, proj)


def _stick_kernel(q_ref, k_ref, v_ref, o_ref, qs_ref, carry_ref, acc_ref, *, tq):
    tk = tq
    n_pairs = C_HEADS // 2
    qi = pl.program_id(1)
    lane = lax.broadcasted_iota(jnp.int32, (tq, LANES), 1)
    low = lane < C_HEAD_DIM
    for p in range(n_pairs):
        q = q_ref[:, p * LANES:(p + 1) * LANES] * jnp.asarray(C_HEAD_DIM ** -0.5, BF16)
        zero = jnp.zeros_like(q)
        qs_ref[p, 0:tq] = jnp.where(low, q, zero)
        qs_ref[p, tq:2 * tq] = jnp.where(low, zero, q)
    carry_ref[...] = jnp.zeros(carry_ref.shape, F32)
    acc_ref[...] = jnp.zeros(acc_ref.shape, F32)

    row = lax.broadcasted_iota(jnp.int32, (2 * tq, tk), 0)
    col = lax.broadcasted_iota(jnp.int32, (2 * tq, tk), 1)
    strict = (row % tq) > col
    uj = lax.broadcasted_iota(jnp.int32, (2 * tk, tk + LANES), 0) % tk
    us = lax.broadcasted_iota(jnp.int32, (2 * tk, tk + LANES), 1)
    w2 = jnp.where(jnp.logical_or(uj > us, us >= tk), 1.0, 0.0).astype(BF16)
    vlow = lax.broadcasted_iota(jnp.int32, (tk, LANES), 1) < C_HEAD_DIM

    def step(kb, masked):
        start = pl.multiple_of(kb * tk, tk)
        pairs = range(n_pairs)
        cols = [slice(p * LANES, (p + 1) * LANES) for p in pairs]
        zs = [_nt_dot(qs_ref[p], k_ref[pl.ds(start, tk), cols[p]]) for p in pairs]
        lbs, rs = [], []
        for p in pairs:
            z = zs[p]
            sp = jnp.log(1.0 + jnp.exp(-jnp.abs(z)))
            lb = jnp.minimum(z, 0.0) - sp
            lk = lb - z
            if masked:
                lk = jnp.where(strict, lk, 0.0)
            hi = lk.astype(BF16)
            lo = (lk - hi.astype(F32)).astype(BF16)
            lbs.append(lb)
            rs.append(_dot(jnp.concatenate([hi, lo], axis=1), w2))
        mx = None
        for p in pairs:
            carry = carry_ref[p]
            a = jnp.exp(lbs[p] + rs[p][:, :tk] + carry)
            if masked:
                a = jnp.where(strict, a, 0.0)
            carry = carry + rs[p][:, tk:]
            carry_ref[p] = carry
            mx = carry if mx is None else jnp.maximum(mx, carry)
            ab = a.astype(BF16)
            vv = v_ref[pl.ds(start, tk), cols[p]]
            zero = jnp.zeros_like(vv)
            vst = jnp.concatenate([jnp.where(vlow, vv, zero), jnp.where(vlow, zero, vv)], axis=0)
            acc_ref[p] += _dot(jnp.concatenate([ab[:tq], ab[tq:]], axis=1), vst)
        return jnp.max(mx)

    def cond(st):
        kb, alive = st
        return jnp.logical_and(kb >= 0, alive > EXP_UNDERFLOW)

    def body(st):
        kb, _ = st
        return kb - 1, step(kb, False)

    lax.while_loop(cond, body, (qi - 1, step(qi, True)))
    for p in range(n_pairs):
        o_ref[:, p * LANES:(p + 1) * LANES] = acc_ref[p].astype(BF16)


def stick_attn(proj, bsz, seq, tq=128):
    nq = seq // tq
    kern = functools.partial(_stick_kernel, tq=tq)
    n_pairs = C_HEADS // 2
    return pl.pallas_call(
        kern,
        grid=(bsz, nq),
        in_specs=[
            pl.BlockSpec((tq, C_WIDTH), lambda b, i: (b * nq + i, 5)),
            pl.BlockSpec((seq, C_WIDTH), lambda b, i: (b, 6)),
            pl.BlockSpec((seq, C_WIDTH), lambda b, i: (b, 7)),
        ],
        out_specs=pl.BlockSpec((tq, C_WIDTH), lambda b, i: (b * nq + i, 0)),
        out_shape=jax.ShapeDtypeStruct((bsz * seq, C_WIDTH), BF16),
        scratch_shapes=[pltpu.VMEM((n_pairs, 2 * tq, LANES), BF16),
                        pltpu.VMEM((n_pairs, 2 * tq, LANES), F32),
                        pltpu.VMEM((n_pairs, tq, LANES), F32)],
        compiler_params=pltpu.CompilerParams(
            dimension_semantics=("arbitrary", "arbitrary"), vmem_limit_bytes=VMEM_LIMIT),
        name="stick_attn",
    )(proj, proj, proj)


def _top2_route(sel, aff):
    def top2_sum(a, b, c, d):
        hi1, lo1 = jnp.maximum(a, b), jnp.minimum(a, b)
        hi2, lo2 = jnp.maximum(c, d), jnp.minimum(c, d)
        return jnp.maximum(hi1, hi2) + jnp.maximum(jnp.minimum(hi1, hi2), jnp.maximum(lo1, lo2))

    scores = [top2_sum(*sel[4 * g:4 * g + 4]) for g in range(N_GROUPS)]
    best = jnp.zeros_like(scores[0], dtype=jnp.int32)
    best_score = scores[0]
    for g in range(1, N_GROUPS):
        better = scores[g] > best_score
        best = jnp.where(better, g, best)
        best_score = jnp.where(better, scores[g], best_score)

    def pick(rows, i):
        out = rows[i]
        for g in range(1, N_GROUPS):
            out = jnp.where(best == g, rows[4 * g + i], out)
        return out

    cs = [pick(sel, i) for i in range(EXPERTS_PER_GROUP)]
    ca = [pick(aff, i) for i in range(EXPERTS_PER_GROUP)]

    def argmax_first(vals, exclude=None):
        idx = jnp.full(vals[0].shape, -1, jnp.int32)
        cur = jnp.full(vals[0].shape, -jnp.inf, F32)
        for i, v in enumerate(vals):
            ok = v > cur
            if exclude is not None:
                ok = jnp.logical_and(ok, exclude != i)
            idx = jnp.where(ok, i, idx)
            cur = jnp.where(ok, v, cur)
        return idx

    i1 = argmax_first(cs)
    i2 = argmax_first(cs, exclude=i1)

    def take(vals, idx):
        out = vals[0]
        for i in range(1, len(vals)):
            out = jnp.where(idx == i, vals[i], out)
        return out

    w1, w2 = take(ca, i1), take(ca, i2)
    tot = w1 + w2
    w1, w2 = w1 / tot, w2 / tot
    e1 = best * EXPERTS_PER_GROUP + i1
    e2 = best * EXPERTS_PER_GROUP + i2
    gates = [jnp.where(e1 == e, w1, 0.0) + jnp.where(e2 == e, w2, 0.0) for e in range(N_EXPERTS)]
    return gates, best


def _merge_kernel(x_ref, u_ref, v_ref, g0_ref, g1_ref, g2_ref, yb_ref, yc_ref,
                  ws_ref, bst_ref, gv_ref, wa_ref, wb_ref, wc_ref, wo_ref, ng_ref, rwt_ref, rb_ref,
                  h_ref, xe_ref, grp_ref, rank_ref, cnt_ref, ya_ref, run_ref, *, tm):
    u = jax.nn.gelu(u_ref[...].astype(F32))
    v = _rms(jax.nn.gelu(v_ref[...].astype(F32)), gv_ref[...]).astype(BF16)

    r = lax.broadcasted_iota(jnp.int32, (BLOCK, BLOCK), 0)
    c = lax.broadcasted_iota(jnp.int32, (BLOCK, BLOCK), 1)
    causal = r >= c
    for g in range(A_GROUPS):
        w = jnp.where(causal, ws_ref[g], 0.0).astype(BF16)
        bias = bst_ref[:, g:g + 1]
        cols = slice(g * LANES, (g + 1) * LANES)
        for ch in range(tm // BLOCK):
            rows = slice(ch * BLOCK, (ch + 1) * BLOCK)
            mixed = _dot(w, v[rows, cols]) + bias
            ya_ref[rows, cols] = (u[rows, cols] * mixed).astype(BF16)

    merged = jax.nn.sigmoid(g0_ref[...].astype(F32)) * _dot(ya_ref[...], wa_ref[...])
    merged += jax.nn.sigmoid(g1_ref[...].astype(F32)) * _dot(yb_ref[...], wb_ref[...])
    merged += jax.nn.sigmoid(g2_ref[...].astype(F32)) * _dot(yc_ref[...], wc_ref[...])
    h = x_ref[...] + _dot(merged.astype(BF16), wo_ref[...])
    h_ref[...] = h

    xn = _rms(h, ng_ref[...])
    xe_ref[:, :D_MODEL] = xn

    xh = xn.astype(BF16)
    xl = (xn - xh.astype(F32)).astype(BF16)
    rw = rwt_ref[...]
    rh = rw.astype(BF16)
    rl = (rw - rh.astype(F32)).astype(BF16)
    logits = _nt_dot(rh, xh) + (_nt_dot(rh, xl) + _nt_dot(rl, xh))
    aff = jax.nn.sigmoid(logits)
    sel = aff + rb_ref[...]
    gates, best = _top2_route([sel[e:e + 1] for e in range(N_EXPERTS)],
                              [aff[e:e + 1] for e in range(N_EXPERTS)])
    gate_rows = jnp.concatenate(gates + [jnp.zeros((LANES - N_EXPERTS, tm), F32)], axis=0)
    xe_ref[:, D_MODEL:] = gate_rows.T

    @pl.when(pl.program_id(0) == 0)
    def _():
        run_ref[...] = jnp.zeros(run_ref.shape, F32)

    gid = lax.broadcasted_iota(jnp.int32, (8, tm), 0)
    member = (gid == best).astype(F32)
    before = lax.broadcasted_iota(jnp.int32, (tm, tm), 0) < lax.broadcasted_iota(jnp.int32, (tm, tm), 1)
    prefix = _dot(member.astype(BF16), before.astype(BF16))
    run = run_ref[...]
    rank = jnp.sum(member * (prefix + run[:, 0:1]), axis=0, keepdims=True)
    rank_ref[...] = rank.astype(jnp.int32)
    grp_ref[...] = best
    run = run + jnp.sum(member, axis=-1, keepdims=True)
    run_ref[...] = run
    cnt_ref[...] = run


def merge(x2, proj, yb, yc, w_s, b_s_t, g_v, wa, wb, wc, wo, norm_g, router_wt, router_b, tm=512):
    t, d = x2.shape
    tm = min(tm, t)
    row = lambda i: (i, 0)
    const2 = lambda i: (0, 0)
    kern = functools.partial(_merge_kernel, tm=tm)
    return pl.pallas_call(
        kern,
        grid=(t // tm,),
        in_specs=[
            pl.BlockSpec((tm, d), row),
            pl.BlockSpec((tm, A_WIDTH), lambda i: (i, 0)),
            pl.BlockSpec((tm, A_WIDTH), lambda i: (i, 1)),
            pl.BlockSpec((tm, d), lambda i: (i, 4)),
            pl.BlockSpec((tm, d), lambda i: (i, 5)),
            pl.BlockSpec((tm, d), lambda i: (i, 6)),
            pl.BlockSpec((tm, A_WIDTH), row),
            pl.BlockSpec((tm, C_WIDTH), row),
            pl.BlockSpec((A_GROUPS, BLOCK, BLOCK), lambda i: (0, 0, 0)),
            pl.BlockSpec((BLOCK, A_GROUPS), const2),
            pl.BlockSpec((1, A_WIDTH), const2),
            pl.BlockSpec((A_WIDTH, d), const2),
            pl.BlockSpec((A_WIDTH, d), const2),
            pl.BlockSpec((C_WIDTH, d), const2),
            pl.BlockSpec((d, d), const2),
            pl.BlockSpec((1, d), const2),
            pl.BlockSpec((N_EXPERTS, d), const2),
            pl.BlockSpec((N_EXPERTS, 1), const2),
        ],
        out_specs=[
            pl.BlockSpec((tm, d), row),
            pl.BlockSpec((tm, XE_WIDTH), row),
            pl.BlockSpec((1, tm), lambda i: (0, i)),
            pl.BlockSpec((1, tm), lambda i: (0, i)),
            pl.BlockSpec((8, LANES), const2),
        ],
        out_shape=[
            jax.ShapeDtypeStruct((t, d), F32),
            jax.ShapeDtypeStruct((t, XE_WIDTH), F32),
            jax.ShapeDtypeStruct((1, t), jnp.int32),
            jax.ShapeDtypeStruct((1, t), jnp.int32),
            jax.ShapeDtypeStruct((8, LANES), F32),
        ],
        scratch_shapes=[pltpu.VMEM((tm, A_WIDTH), BF16), pltpu.VMEM((8, LANES), F32)],
        compiler_params=pltpu.CompilerParams(
            dimension_semantics=("arbitrary",), vmem_limit_bytes=VMEM_LIMIT),
        name="merge",
    )(x2, proj, proj, proj, proj, proj, yb, yc, w_s, b_s_t, g_v.reshape(1, A_WIDTH),
      wa, wb, wc, wo, norm_g.reshape(1, d), router_wt, router_b.reshape(N_EXPERTS, 1))


def _route_tables(cnt, grp, rank, t, rt):
    cnt = cnt[:N_GROUPS, 0].astype(jnp.int32)
    padded = (cnt + rt - 1) // rt * rt
    off_end = jnp.cumsum(padded)
    off = off_end - padded
    g = grp.reshape(t)
    pos = rank.reshape(t)
    for k in range(N_GROUPS):
        pos = pos + jnp.where(g == k, off[k], 0)
    starts = jnp.arange(t // rt + N_GROUPS, dtype=jnp.int32) * rt
    tile_group = jnp.minimum(jnp.sum((starts[:, None] >= off_end[None, :]).astype(jnp.int32), axis=1),
                             N_GROUPS - 1)
    tile_valid = (starts < off_end[N_GROUPS - 1]).astype(jnp.int32)
    return pos, tile_group, tile_valid


def _row_copy(src_ref, src_row, dst_ref, dst_row, sem):
    return pltpu.make_async_copy(src_ref.at[pl.ds(src_row, 1)], dst_ref.at[pl.ds(dst_row, 1)], sem)


def _dispatch_kernel(pos_ref, xe_ref, xs_init_ref, xs_ref, sem, *, tm):
    del xs_init_ref
    base = pl.program_id(0) * tm

    def issue(r, c):
        _row_copy(xe_ref, r, xs_ref, pos_ref[base + r], sem).start()
        return c

    def drain(r, c):
        _row_copy(xe_ref, r, xs_ref, pos_ref[base + r], sem).wait()
        return c

    lax.fori_loop(0, tm, issue, 0)
    lax.fori_loop(0, tm, drain, 0)


def moe_dispatch(pos, xe, t_pad, tm=512):
    t, w = xe.shape
    tm = min(tm, t)
    kern = functools.partial(_dispatch_kernel, tm=tm)
    return pl.pallas_call(
        kern,
        grid_spec=pltpu.PrefetchScalarGridSpec(
            num_scalar_prefetch=1,
            grid=(t // tm,),
            in_specs=[pl.BlockSpec((tm, w), lambda i, pos: (i, 0)),
                      pl.BlockSpec(memory_space=pl.ANY)],
            out_specs=pl.BlockSpec(memory_space=pl.ANY),
            scratch_shapes=[pltpu.SemaphoreType.DMA],
        ),
        out_shape=jax.ShapeDtypeStruct((t_pad, w), F32),
        input_output_aliases={2: 0},
        compiler_params=pltpu.CompilerParams(
            dimension_semantics=("arbitrary",), vmem_limit_bytes=VMEM_LIMIT),
        name="moe_dispatch",
    )(pos, xe, jnp.zeros((t_pad, w), F32))


def _moe_ffn_kernel(tg_ref, tv_ref, xs_ref, w1_ref, w3_ref, w2_ref, ys_ref, acc_ref):
    i = pl.program_id(0)
    e = pl.program_id(1)
    valid = tv_ref[i] == 1
    last = e == EXPERTS_PER_GROUP - 1

    @pl.when(valid)
    def _():
        @pl.when(e == 0)
        def _():
            acc_ref[...] = jnp.zeros(acc_ref.shape, F32)

        x = xs_ref[:, :D_MODEL].astype(BF16)
        ext = xs_ref[:, D_MODEL:]
        lane = lax.broadcasted_iota(jnp.int32, ext.shape, 1)
        expert = tg_ref[i] * EXPERTS_PER_GROUP + e
        gcol = jnp.sum(jnp.where(lane == expert, ext, 0.0), axis=-1, keepdims=True)
        a = _dot(x, w1_ref[...])
        b = _dot(x, w3_ref[...])
        hh = (jax.nn.silu(a) * b) * gcol
        acc_ref[...] += _dot(hh.astype(BF16), w2_ref[...])

        @pl.when(last)
        def _():
            ys_ref[...] = acc_ref[...]

    @pl.when(jnp.logical_and(jnp.logical_not(valid), last))
    def _():
        ys_ref[...] = jnp.zeros(ys_ref.shape, F32)


def moe_ffn(tile_group, tile_valid, xs, w1, w3, w2, rt):
    t_pad, w = xs.shape
    d = D_MODEL

    def widx(i, e, tg, tv):
        return (tg[i] * EXPERTS_PER_GROUP + jnp.where(tv[i] == 1, e, EXPERTS_PER_GROUP - 1), 0, 0)

    return pl.pallas_call(
        _moe_ffn_kernel,
        grid_spec=pltpu.PrefetchScalarGridSpec(
            num_scalar_prefetch=2,
            grid=(t_pad // rt, EXPERTS_PER_GROUP),
            in_specs=[pl.BlockSpec((rt, w), lambda i, e, tg, tv: (i, 0)),
                      pl.BlockSpec((None, d, D_FF_EXPERT), widx),
                      pl.BlockSpec((None, d, D_FF_EXPERT), widx),
                      pl.BlockSpec((None, D_FF_EXPERT, d), widx)],
            out_specs=pl.BlockSpec((rt, d), lambda i, e, tg, tv: (i, 0)),
            scratch_shapes=[pltpu.VMEM((rt, d), F32)],
        ),
        out_shape=jax.ShapeDtypeStruct((t_pad, d), F32),
        compiler_params=pltpu.CompilerParams(
            dimension_semantics=("arbitrary", "arbitrary"), vmem_limit_bytes=VMEM_LIMIT),
        name="moe_ffn",
    )(tile_group, tile_valid, xs, w1, w3, w2)


def _combine_kernel(pos_ref, h_ref, ys_ref, fg_ref, o_ref, ybuf_ref, sem, *, tm, final_norm):
    base = pl.program_id(0) * tm

    def issue(r, c):
        _row_copy(ys_ref, pos_ref[base + r], ybuf_ref, r, sem).start()
        return c

    def drain(r, c):
        _row_copy(ys_ref, pos_ref[base + r], ybuf_ref, r, sem).wait()
        return c

    lax.fori_loop(0, tm, issue, 0)
    lax.fori_loop(0, tm, drain, 0)
    y = h_ref[...] + ybuf_ref[...]
    if final_norm:
        y = _rms(y, fg_ref[...])
    o_ref[...] = y


def moe_combine(pos, h, ys, final_gain, final_norm, tm=512):
    t, d = h.shape
    tm = min(tm, t)
    kern = functools.partial(_combine_kernel, tm=tm, final_norm=final_norm)
    return pl.pallas_call(
        kern,
        grid_spec=pltpu.PrefetchScalarGridSpec(
            num_scalar_prefetch=1,
            grid=(t // tm,),
            in_specs=[pl.BlockSpec((tm, d), lambda i, pos: (i, 0)),
                      pl.BlockSpec(memory_space=pl.ANY),
                      pl.BlockSpec((1, d), lambda i, pos: (0, 0))],
            out_specs=pl.BlockSpec((tm, d), lambda i, pos: (i, 0)),
            scratch_shapes=[pltpu.VMEM((tm, d), F32), pltpu.SemaphoreType.DMA],
        ),
        out_shape=jax.ShapeDtypeStruct((t, d), F32),
        compiler_params=pltpu.CompilerParams(
            dimension_semantics=("arbitrary",), vmem_limit_bytes=VMEM_LIMIT),
        name="moe_combine",
    )(pos, h, ys, final_gain.reshape(1, d))


def grouped_moe(h, xe, grp, rank, cnt, w1, w3, w2, final_gain, final_norm, rt=512):
    t = h.shape[0]
    rt = min(rt, t)
    pos, tile_group, tile_valid = _route_tables(cnt, grp, rank, t, rt)
    xs = moe_dispatch(pos, xe, t + N_GROUPS * rt)
    ys = moe_ffn(tile_group, tile_valid, xs, w1, w3, w2, rt)
    return moe_combine(pos, h, ys, final_gain, final_norm)


def kernel(x, norm_mix_gain, w_in, gmlp_w_s, gmlp_b_s, gmlp_v_gain, diff_lambda, diff_subln_gain,
           w_up_a, w_up_b, w_up_c, w_out, norm_ffn_gain, router_w, router_bias,
           moe_w1, moe_w3, moe_w2, final_gain):
    bsz, seq, d = x.shape
    depth = w_in.shape[0]
    x2 = x.reshape(bsz * seq, d)
    router_wt = router_w.T
    for l in range(depth):
        lambda_init = 0.8 - 0.6 * math.exp(-0.3 * l)
        proj = in_proj(x2, norm_mix_gain[l], w_in[l].astype(BF16))
        yb = diff_attn(proj, diff_lambda[l], diff_subln_gain[l], bsz, seq, lambda_init)
        yc = stick_attn(proj, bsz, seq)
        h, xe, grp, rank, cnt = merge(x2, proj, yb, yc, gmlp_w_s[l], gmlp_b_s[l].T, gmlp_v_gain[l],
                                      w_up_a[l].astype(BF16), w_up_b[l].astype(BF16), w_up_c[l].astype(BF16),
                                      w_out[l].astype(BF16), norm_ffn_gain[l], router_wt, router_bias)
        x2 = grouped_moe(h, xe, grp, rank, cnt, moe_w1[l].astype(BF16), moe_w3[l].astype(BF16),
                         moe_w2[l].astype(BF16), final_gain, final_norm=(l == depth - 1))
    return x2.reshape(bsz, seq, d)
```

```python
import functools
import math

import jax
import jax.numpy as jnp
from jax import lax
from jax.experimental import pallas as pl
from jax.experimental.pallas import tpu as pltpu

D_MODEL = 1024
BLOCK = 128
A_GROUPS = 4
A_WIDTH = 512
B_HEADS = 4
B_QK_DIM = 64
B_V_DIM = 128
C_HEADS = 8
C_HEAD_DIM = 64
C_WIDTH = 512
N_EXPERTS = 16
N_GROUPS = 4
EXPERTS_PER_GROUP = 4
D_FF_EXPERT = 512
IN_WIDTH = 7168
XE_WIDTH = D_MODEL + 128
EPS = 1e-6

LANES = 128
VMEM_LIMIT = 56 * 1024 * 1024
F32 = jnp.float32
BF16 = jnp.bfloat16
NEG_BIG = -1e30
EXP_UNDERFLOW = -104.0

COL_QB, COL_KB, COL_VB = 8, 12, 16


def _nt_dot(a, b):
    return lax.dot_general(a, b, (((1,), (1,)), ((), ())), preferred_element_type=F32)


def _dot(a, b):
    return jnp.dot(a, b, preferred_element_type=F32)


def _rms(xf, gain):
    return xf * lax.rsqrt(jnp.mean(xf * xf, axis=-1, keepdims=True) + EPS) * gain


def _in_proj_kernel(x_ref, g_ref, w_ref, o_ref, xn_ref):
    @pl.when(pl.program_id(1) == 0)
    def _():
        xn_ref[...] = _rms(x_ref[...], g_ref[...]).astype(BF16)

    o_ref[...] = _dot(xn_ref[...], w_ref[...]).astype(BF16)


def in_proj(x2, gain, w_bf16, tm=1024, tn=1024):
    t, d = x2.shape
    n = w_bf16.shape[1]
    tm = min(tm, t)
    return pl.pallas_call(
        _in_proj_kernel,
        grid=(t // tm, n // tn),
        in_specs=[
            pl.BlockSpec((tm, d), lambda i, j: (i, 0)),
            pl.BlockSpec((1, d), lambda i, j: (0, 0)),
            pl.BlockSpec((d, tn), lambda i, j: (0, j)),
        ],
        out_specs=pl.BlockSpec((tm, tn), lambda i, j: (i, j)),
        out_shape=jax.ShapeDtypeStruct((t, n), BF16),
        scratch_shapes=[pltpu.VMEM((tm, d), BF16)],
        compiler_params=pltpu.CompilerParams(
            dimension_semantics=("arbitrary", "arbitrary"), vmem_limit_bytes=VMEM_LIMIT),
        name="in_proj",
    )(x2, gain.reshape(1, d), w_bf16)


def _diff_attn_kernel(lam_ref, sg_ref, q_ref, k_ref, v_ref, o_ref, m_ref, l_ref, acc_ref,
                      *, tq, tk, lambda_init):
    h = pl.program_id(1)
    qi = pl.program_id(2)
    nc = tk // LANES
    slope = jnp.where(h == 0, 0.25, jnp.where(h == 1, 0.0625, jnp.where(h == 2, 0.015625, 0.00390625)))
    slope = slope.astype(F32)

    lane = lax.broadcasted_iota(jnp.int32, (tq, LANES), 1)
    q = q_ref[...] * jnp.asarray(B_QK_DIM ** -0.5, BF16)
    zero = jnp.zeros_like(q)
    qs = (jnp.where(lane < B_QK_DIM, q, zero), jnp.where(lane >= B_QK_DIM, q, zero))

    m_ref[...] = jnp.full(m_ref.shape, NEG_BIG, F32)
    l_ref[...] = jnp.zeros(l_ref.shape, F32)
    acc_ref[...] = jnp.zeros(acc_ref.shape, F32)

    kcol = lax.broadcasted_iota(jnp.int32, (1, LANES), 1)
    row_minus_col = lax.broadcasted_iota(jnp.int32, (tq, LANES), 0) - lane
    q0 = qi * tq

    def block(kb, masked):
        k0 = kb * tk
        kk = k_ref[pl.ds(pl.multiple_of(k0, tk), tk), :]
        vv = v_ref[pl.ds(pl.multiple_of(k0, tk), tk), :]
        biases = [slope * (kcol + (k0 - q0 + c * LANES)).astype(F32) for c in range(nc)]
        ss = [_nt_dot(qs[mi], kk) for mi in range(2)]
        for mi in range(2):
            s = ss[mi]
            cols = []
            for c in range(nc):
                sc = s[:, c * LANES:(c + 1) * LANES] + biases[c]
                if masked:
                    sc = jnp.where(row_minus_col >= (k0 - q0 + c * LANES), sc, NEG_BIG)
                cols.append(sc)
            mx = cols[0]
            for sc in cols[1:]:
                mx = jnp.maximum(mx, sc)
            m_prev = m_ref[mi]
            m_new = jnp.maximum(m_prev, jnp.max(mx, axis=-1, keepdims=True))
            alpha = jnp.exp(m_prev - m_new)
            ps = [jnp.exp(sc - m_new) for sc in cols]
            psum = ps[0]
            for pc in ps[1:]:
                psum = psum + pc
            l_ref[mi] = alpha * l_ref[mi] + psum
            p = jnp.concatenate([pc.astype(BF16) for pc in ps], axis=1)
            acc_ref[mi] = alpha * acc_ref[mi] + _dot(p, vv)
            m_ref[mi] = m_new

    n_full = (qi * tq) // tk

    def body(kb, c):
        block(kb, False)
        return c

    lax.fori_loop(0, n_full, body, 0)
    for d in range(tq // tk):
        block(n_full + d, True)

    lp = lam_ref[...]
    s1 = jnp.sum(lp[0:1] * lp[1:2], axis=-1, keepdims=True)
    s2 = jnp.sum(lp[2:3] * lp[3:4], axis=-1, keepdims=True)
    lam = jnp.exp(s1) - jnp.exp(s2) + lambda_init
    l0 = jnp.sum(l_ref[0], axis=-1, keepdims=True)
    l1 = jnp.sum(l_ref[1], axis=-1, keepdims=True)
    o = acc_ref[0] / l0 - lam * (acc_ref[1] / l1)
    o_ref[...] = (_rms(o, sg_ref[...]) * (1.0 - lambda_init)).astype(BF16)


def diff_attn(proj, lam_params, subln_g, bsz, seq, lambda_init, tq=512, tk=512):
    tq = min(tq, seq)
    tk = min(tk, tq)
    nq = seq // tq
    kern = functools.partial(_diff_attn_kernel, tq=tq, tk=tk, lambda_init=lambda_init)
    return pl.pallas_call(
        kern,
        grid=(bsz, B_HEADS, nq),
        in_specs=[
            pl.BlockSpec((4, B_QK_DIM), lambda b, h, i: (0, 0)),
            pl.BlockSpec((1, B_V_DIM), lambda b, h, i: (0, 0)),
            pl.BlockSpec((tq, LANES), lambda b, h, i: (b * nq + i, COL_QB + h)),
            pl.BlockSpec((seq, LANES), lambda b, h, i: (b, COL_KB + h)),
            pl.BlockSpec((seq, LANES), lambda b, h, i: (b, COL_VB + h)),
        ],
        out_specs=pl.BlockSpec((tq, LANES), lambda b, h, i: (b * nq + i, h)),
        out_shape=jax.ShapeDtypeStruct((bsz * seq, B_HEADS * B_V_DIM), BF16),
        scratch_shapes=[pltpu.VMEM((2, tq, LANES), F32), pltpu.VMEM((2, tq, LANES), F32),
                        pltpu.VMEM((2, tq, B_V_DIM), F32)],
        compiler_params=pltpu.CompilerParams(
            dimension_semantics=("arbitrary", "arbitrary", "arbitrary"), vmem_limit_bytes=VMEM_LIMIT),
        name="diff_attn",
    )(lam_params, subln_g.reshape(1, B_V_DIM), proj, proj, proj)


def _stick_kernel(q_ref, k_ref, v_ref, o_ref, qs_ref, carry_ref, acc_ref, *, tq):
    tk = tq
    n_pairs = C_HEADS // 2
    qi = pl.program_id(1)
    lane = lax.broadcasted_iota(jnp.int32, (tq, LANES), 1)
    low = lane < C_HEAD_DIM
    for p in range(n_pairs):
        q = q_ref[:, p * LANES:(p + 1) * LANES] * jnp.asarray(C_HEAD_DIM ** -0.5, BF16)
        zero = jnp.zeros_like(q)
        qs_ref[p, 0:tq] = jnp.where(low, q, zero)
        qs_ref[p, tq:2 * tq] = jnp.where(low, zero, q)
    carry_ref[...] = jnp.zeros(carry_ref.shape, F32)
    acc_ref[...] = jnp.zeros(acc_ref.shape, F32)

    row = lax.broadcasted_iota(jnp.int32, (2 * tq, tk), 0)
    col = lax.broadcasted_iota(jnp.int32, (2 * tq, tk), 1)
    strict = (row % tq) > col
    uj = lax.broadcasted_iota(jnp.int32, (2 * tk, tk + LANES), 0) % tk
    us = lax.broadcasted_iota(jnp.int32, (2 * tk, tk + LANES), 1)
    w2 = jnp.where(jnp.logical_or(uj > us, us >= tk), 1.0, 0.0).astype(BF16)
    vlow = lax.broadcasted_iota(jnp.int32, (tk, LANES), 1) < C_HEAD_DIM

    def step(kb, masked):
        start = pl.multiple_of(kb * tk, tk)
        pairs = range(n_pairs)
        cols = [slice(p * LANES, (p + 1) * LANES) for p in pairs]
        zs = [_nt_dot(qs_ref[p], k_ref[pl.ds(start, tk), cols[p]]) for p in pairs]
        lbs, rs = [], []
        for p in pairs:
            z = zs[p]
            sp = jnp.log(1.0 + jnp.exp(-jnp.abs(z)))
            lb = jnp.minimum(z, 0.0) - sp
            lk = lb - z
            if masked:
                lk = jnp.where(strict, lk, 0.0)
            hi = lk.astype(BF16)
            lo = (lk - hi.astype(F32)).astype(BF16)
            lbs.append(lb)
            rs.append(_dot(jnp.concatenate([hi, lo], axis=1), w2))
        mx = None
        for p in pairs:
            carry = carry_ref[p]
            a = jnp.exp(lbs[p] + rs[p][:, :tk] + carry)
            if masked:
                a = jnp.where(strict, a, 0.0)
            carry = carry + rs[p][:, tk:]
            carry_ref[p] = carry
            mx = carry if mx is None else jnp.maximum(mx, carry)
            ab = a.astype(BF16)
            vv = v_ref[pl.ds(start, tk), cols[p]]
            zero = jnp.zeros_like(vv)
            vst = jnp.concatenate([jnp.where(vlow, vv, zero), jnp.where(vlow, zero, vv)], axis=0)
            acc_ref[p] += _dot(jnp.concatenate([ab[:tq], ab[tq:]], axis=1), vst)
        return jnp.max(mx)

    def cond(st):
        kb, alive = st
        return jnp.logical_and(kb >= 0, alive > EXP_UNDERFLOW)

    def body(st):
        kb, _ = st
        return kb - 1, step(kb, False)

    lax.while_loop(cond, body, (qi - 1, step(qi, True)))
    for p in range(n_pairs):
        o_ref[:, p * LANES:(p + 1) * LANES] = acc_ref[p].astype(BF16)


def stick_attn(proj, bsz, seq, tq=128):
    nq = seq // tq
    kern = functools.partial(_stick_kernel, tq=tq)
    n_pairs = C_HEADS // 2
    return pl.pallas_call(
        kern,
        grid=(bsz, nq),
        in_specs=[
            pl.BlockSpec((tq, C_WIDTH), lambda b, i: (b * nq + i, 5)),
            pl.BlockSpec((seq, C_WIDTH), lambda b, i: (b, 6)),
            pl.BlockSpec((seq, C_WIDTH), lambda b, i: (b, 7)),
        ],
        out_specs=pl.BlockSpec((tq, C_WIDTH), lambda b, i: (b * nq + i, 0)),
        out_shape=jax.ShapeDtypeStruct((bsz * seq, C_WIDTH), BF16),
        scratch_shapes=[pltpu.VMEM((n_pairs, 2 * tq, LANES), BF16),
                        pltpu.VMEM((n_pairs, 2 * tq, LANES), F32),
                        pltpu.VMEM((n_pairs, tq, LANES), F32)],
        compiler_params=pltpu.CompilerParams(
            dimension_semantics=("arbitrary", "arbitrary"), vmem_limit_bytes=VMEM_LIMIT),
        name="stick_attn",
    )(proj, proj, proj)


def _top2_route(sel, aff):
    def top2_sum(a, b, c, d):
        hi1, lo1 = jnp.maximum(a, b), jnp.minimum(a, b)
        hi2, lo2 = jnp.maximum(c, d), jnp.minimum(c, d)
        return jnp.maximum(hi1, hi2) + jnp.maximum(jnp.minimum(hi1, hi2), jnp.maximum(lo1, lo2))

    scores = [top2_sum(*sel[4 * g:4 * g + 4]) for g in range(N_GROUPS)]
    best = jnp.zeros_like(scores[0], dtype=jnp.int32)
    best_score = scores[0]
    for g in range(1, N_GROUPS):
        better = scores[g] > best_score
        best = jnp.where(better, g, best)
        best_score = jnp.where(better, scores[g], best_score)

    def pick(rows, i):
        out = rows[i]
        for g in range(1, N_GROUPS):
            out = jnp.where(best == g, rows[4 * g + i], out)
        return out

    cs = [pick(sel, i) for i in range(EXPERTS_PER_GROUP)]
    ca = [pick(aff, i) for i in range(EXPERTS_PER_GROUP)]

    def argmax_first(vals, exclude=None):
        idx = jnp.full(vals[0].shape, -1, jnp.int32)
        cur = jnp.full(vals[0].shape, -jnp.inf, F32)
        for i, v in enumerate(vals):
            ok = v > cur
            if exclude is not None:
                ok = jnp.logical_and(ok, exclude != i)
            idx = jnp.where(ok, i, idx)
            cur = jnp.where(ok, v, cur)
        return idx

    i1 = argmax_first(cs)
    i2 = argmax_first(cs, exclude=i1)

    def take(vals, idx):
        out = vals[0]
        for i in range(1, len(vals)):
            out = jnp.where(idx == i, vals[i], out)
        return out

    w1, w2 = take(ca, i1), take(ca, i2)
    tot = w1 + w2
    w1, w2 = w1 / tot, w2 / tot
    e1 = best * EXPERTS_PER_GROUP + i1
    e2 = best * EXPERTS_PER_GROUP + i2
    gates = [jnp.where(e1 == e, w1, 0.0) + jnp.where(e2 == e, w2, 0.0) for e in range(N_EXPERTS)]
    return gates, best


def _merge_kernel(x_ref, u_ref, v_ref, g0_ref, g1_ref, g2_ref, yb_ref, yc_ref,
                  ws_ref, bst_ref, gv_ref, wa_ref, wb_ref, wc_ref, wo_ref, ng_ref, rwt_ref, rb_ref,
                  h_ref, xe_ref, grp_ref, rank_ref, cnt_ref, ya_ref, run_ref, *, tm):
    u = jax.nn.gelu(u_ref[...].astype(F32))
    v = _rms(jax.nn.gelu(v_ref[...].astype(F32)), gv_ref[...]).astype(BF16)

    r = lax.broadcasted_iota(jnp.int32, (BLOCK, BLOCK), 0)
    c = lax.broadcasted_iota(jnp.int32, (BLOCK, BLOCK), 1)
    causal = r >= c
    for g in range(A_GROUPS):
        w = jnp.where(causal, ws_ref[g], 0.0).astype(BF16)
        bias = bst_ref[:, g:g + 1]
        cols = slice(g * LANES, (g + 1) * LANES)
        for ch in range(tm // BLOCK):
            rows = slice(ch * BLOCK, (ch + 1) * BLOCK)
            mixed = _dot(w, v[rows, cols]) + bias
            ya_ref[rows, cols] = (u[rows, cols] * mixed).astype(BF16)

    merged = jax.nn.sigmoid(g0_ref[...].astype(F32)) * _dot(ya_ref[...], wa_ref[...])
    merged += jax.nn.sigmoid(g1_ref[...].astype(F32)) * _dot(yb_ref[...], wb_ref[...])
    merged += jax.nn.sigmoid(g2_ref[...].astype(F32)) * _dot(yc_ref[...], wc_ref[...])
    h = x_ref[...] + _dot(merged.astype(BF16), wo_ref[...])
    h_ref[...] = h

    xn = _rms(h, ng_ref[...])
    xe_ref[:, :D_MODEL] = xn

    xh = xn.astype(BF16)
    xl = (xn - xh.astype(F32)).astype(BF16)
    rw = rwt_ref[...]
    rh = rw.astype(BF16)
    rl = (rw - rh.astype(F32)).astype(BF16)
    logits = _nt_dot(rh, xh) + (_nt_dot(rh, xl) + _nt_dot(rl, xh))
    aff = jax.nn.sigmoid(logits)
    sel = aff + rb_ref[...]
    gates, best = _top2_route([sel[e:e + 1] for e in range(N_EXPERTS)],
                              [aff[e:e + 1] for e in range(N_EXPERTS)])
    gate_rows = jnp.concatenate(gates + [jnp.zeros((LANES - N_EXPERTS, tm), F32)], axis=0)
    xe_ref[:, D_MODEL:] = gate_rows.T

    @pl.when(pl.program_id(0) == 0)
    def _():
        run_ref[...] = jnp.zeros(run_ref.shape, F32)

    gid = lax.broadcasted_iota(jnp.int32, (8, tm), 0)
    member = (gid == best).astype(F32)
    before = lax.broadcasted_iota(jnp.int32, (tm, tm), 0) < lax.broadcasted_iota(jnp.int32, (tm, tm), 1)
    prefix = _dot(member.astype(BF16), before.astype(BF16))
    run = run_ref[...]
    rank = jnp.sum(member * (prefix + run[:, 0:1]), axis=0, keepdims=True)
    rank_ref[...] = rank.astype(jnp.int32)
    grp_ref[...] = best
    run = run + jnp.sum(member, axis=-1, keepdims=True)
    run_ref[...] = run
    cnt_ref[...] = run


def merge(x2, proj, yb, yc, w_s, b_s_t, g_v, wa, wb, wc, wo, norm_g, router_wt, router_b, tm=512):
    t, d = x2.shape
    tm = min(tm, t)
    row = lambda i: (i, 0)
    const2 = lambda i: (0, 0)
    kern = functools.partial(_merge_kernel, tm=tm)
    return pl.pallas_call(
        kern,
        grid=(t // tm,),
        in_specs=[
            pl.BlockSpec((tm, d), row),
            pl.BlockSpec((tm, A_WIDTH), lambda i: (i, 0)),
            pl.BlockSpec((tm, A_WIDTH), lambda i: (i, 1)),
            pl.BlockSpec((tm, d), lambda i: (i, 4)),
            pl.BlockSpec((tm, d), lambda i: (i, 5)),
            pl.BlockSpec((tm, d), lambda i: (i, 6)),
            pl.BlockSpec((tm, A_WIDTH), row),
            pl.BlockSpec((tm, C_WIDTH), row),
            pl.BlockSpec((A_GROUPS, BLOCK, BLOCK), lambda i: (0, 0, 0)),
            pl.BlockSpec((BLOCK, A_GROUPS), const2),
            pl.BlockSpec((1, A_WIDTH), const2),
            pl.BlockSpec((A_WIDTH, d), const2),
            pl.BlockSpec((A_WIDTH, d), const2),
            pl.BlockSpec((C_WIDTH, d), const2),
            pl.BlockSpec((d, d), const2),
            pl.BlockSpec((1, d), const2),
            pl.BlockSpec((N_EXPERTS, d), const2),
            pl.BlockSpec((N_EXPERTS, 1), const2),
        ],
        out_specs=[
            pl.BlockSpec((tm, d), row),
            pl.BlockSpec((tm, XE_WIDTH), row),
            pl.BlockSpec((1, tm), lambda i: (0, i)),
            pl.BlockSpec((1, tm), lambda i: (0, i)),
            pl.BlockSpec((8, LANES), const2),
        ],
        out_shape=[
            jax.ShapeDtypeStruct((t, d), F32),
            jax.ShapeDtypeStruct((t, XE_WIDTH), F32),
            jax.ShapeDtypeStruct((1, t), jnp.int32),
            jax.ShapeDtypeStruct((1, t), jnp.int32),
            jax.ShapeDtypeStruct((8, LANES), F32),
        ],
        scratch_shapes=[pltpu.VMEM((tm, A_WIDTH), BF16), pltpu.VMEM((8, LANES), F32)],
        compiler_params=pltpu.CompilerParams(
            dimension_semantics=("arbitrary",), vmem_limit_bytes=VMEM_LIMIT),
        name="merge",
    )(x2, proj, proj, proj, proj, proj, yb, yc, w_s, b_s_t, g_v.reshape(1, A_WIDTH),
      wa, wb, wc, wo, norm_g.reshape(1, d), router_wt, router_b.reshape(N_EXPERTS, 1))


def _route_tables(cnt, grp, rank, t, rt):
    cnt = cnt[:N_GROUPS, 0].astype(jnp.int32)
    padded = (cnt + rt - 1) // rt * rt
    off_end = jnp.cumsum(padded)
    off = off_end - padded
    g = grp.reshape(t)
    pos = rank.reshape(t)
    for k in range(N_GROUPS):
        pos = pos + jnp.where(g == k, off[k], 0)
    starts = jnp.arange(t // rt + N_GROUPS, dtype=jnp.int32) * rt
    tile_group = jnp.minimum(jnp.sum((starts[:, None] >= off_end[None, :]).astype(jnp.int32), axis=1),
                             N_GROUPS - 1)
    tile_valid = (starts < off_end[N_GROUPS - 1]).astype(jnp.int32)
    return pos, tile_group, tile_valid


ROW_DMA_UNROLL = 8


def _row_copy(src_ref, src_row, dst_ref, dst_row, sem):
    return pltpu.make_async_copy(src_ref.at[pl.ds(src_row, 1)], dst_ref.at[pl.ds(dst_row, 1)], sem)


def _dispatch_kernel(pos_ref, xe_ref, xs_init_ref, xs_ref, sem, *, tm):
    del xs_init_ref
    base = pl.program_id(0) * tm

    def issue(r, c):
        _row_copy(xe_ref, r, xs_ref, pos_ref[base + r], sem).start()
        return c

    lax.fori_loop(0, tm, issue, 0, unroll=ROW_DMA_UNROLL)
    pltpu.make_async_copy(xe_ref, xs_ref.at[pl.ds(0, tm)], sem).wait()


def moe_dispatch(pos, xe, t_pad, tm=512):
    t, w = xe.shape
    tm = min(tm, t)
    kern = functools.partial(_dispatch_kernel, tm=tm)
    return pl.pallas_call(
        kern,
        grid_spec=pltpu.PrefetchScalarGridSpec(
            num_scalar_prefetch=1,
            grid=(t // tm,),
            in_specs=[pl.BlockSpec((tm, w), lambda i, pos: (i, 0)),
                      pl.BlockSpec(memory_space=pl.ANY)],
            out_specs=pl.BlockSpec(memory_space=pl.ANY),
            scratch_shapes=[pltpu.SemaphoreType.DMA],
        ),
        out_shape=jax.ShapeDtypeStruct((t_pad, w), F32),
        input_output_aliases={2: 0},
        compiler_params=pltpu.CompilerParams(
            dimension_semantics=("arbitrary",), vmem_limit_bytes=VMEM_LIMIT),
        name="moe_dispatch",
    )(pos, xe, jnp.zeros((t_pad, w), F32))


def _moe_ffn_kernel(tg_ref, tv_ref, xs_ref, w1_ref, w3_ref, w2_ref, ys_ref, acc_ref):
    i = pl.program_id(0)
    e = pl.program_id(1)
    valid = tv_ref[i] == 1
    last = e == EXPERTS_PER_GROUP - 1

    @pl.when(valid)
    def _():
        @pl.when(e == 0)
        def _():
            acc_ref[...] = jnp.zeros(acc_ref.shape, F32)

        x = xs_ref[:, :D_MODEL].astype(BF16)
        ext = xs_ref[:, D_MODEL:]
        lane = lax.broadcasted_iota(jnp.int32, ext.shape, 1)
        expert = tg_ref[i] * EXPERTS_PER_GROUP + e
        gcol = jnp.sum(jnp.where(lane == expert, ext, 0.0), axis=-1, keepdims=True)
        a = _dot(x, w1_ref[...])
        b = _dot(x, w3_ref[...])
        hh = (jax.nn.silu(a) * b) * gcol
        acc_ref[...] += _dot(hh.astype(BF16), w2_ref[...])

        @pl.when(last)
        def _():
            ys_ref[...] = acc_ref[...]

    @pl.when(jnp.logical_and(jnp.logical_not(valid), last))
    def _():
        ys_ref[...] = jnp.zeros(ys_ref.shape, F32)


def moe_ffn(tile_group, tile_valid, xs, w1, w3, w2, rt):
    t_pad, w = xs.shape
    d = D_MODEL

    def widx(i, e, tg, tv):
        return (tg[i] * EXPERTS_PER_GROUP + jnp.where(tv[i] == 1, e, EXPERTS_PER_GROUP - 1), 0, 0)

    return pl.pallas_call(
        _moe_ffn_kernel,
        grid_spec=pltpu.PrefetchScalarGridSpec(
            num_scalar_prefetch=2,
            grid=(t_pad // rt, EXPERTS_PER_GROUP),
            in_specs=[pl.BlockSpec((rt, w), lambda i, e, tg, tv: (i, 0)),
                      pl.BlockSpec((None, d, D_FF_EXPERT), widx),
                      pl.BlockSpec((None, d, D_FF_EXPERT), widx),
                      pl.BlockSpec((None, D_FF_EXPERT, d), widx)],
            out_specs=pl.BlockSpec((rt, d), lambda i, e, tg, tv: (i, 0)),
            scratch_shapes=[pltpu.VMEM((rt, d), F32)],
        ),
        out_shape=jax.ShapeDtypeStruct((t_pad, d), F32),
        compiler_params=pltpu.CompilerParams(
            dimension_semantics=("arbitrary", "arbitrary"), vmem_limit_bytes=VMEM_LIMIT),
        name="moe_ffn",
    )(tile_group, tile_valid, xs, w1, w3, w2)


def _combine_kernel(pos_ref, h_ref, ys_ref, fg_ref, o_ref, ybuf_ref, sem, *, tm, final_norm):
    base = pl.program_id(0) * tm

    def issue(r, c):
        _row_copy(ys_ref, pos_ref[base + r], ybuf_ref, r, sem).start()
        return c

    lax.fori_loop(0, tm, issue, 0, unroll=ROW_DMA_UNROLL)
    pltpu.make_async_copy(ys_ref.at[pl.ds(0, tm)], ybuf_ref, sem).wait()
    y = h_ref[...] + ybuf_ref[...]
    if final_norm:
        y = _rms(y, fg_ref[...])
    o_ref[...] = y


def moe_combine(pos, h, ys, final_gain, final_norm, tm=512):
    t, d = h.shape
    tm = min(tm, t)
    kern = functools.partial(_combine_kernel, tm=tm, final_norm=final_norm)
    return pl.pallas_call(
        kern,
        grid_spec=pltpu.PrefetchScalarGridSpec(
            num_scalar_prefetch=1,
            grid=(t // tm,),
            in_specs=[pl.BlockSpec((tm, d), lambda i, pos: (i, 0)),
                      pl.BlockSpec(memory_space=pl.ANY),
                      pl.BlockSpec((1, d), lambda i, pos: (0, 0))],
            out_specs=pl.BlockSpec((tm, d), lambda i, pos: (i, 0)),
            scratch_shapes=[pltpu.VMEM((tm, d), F32), pltpu.SemaphoreType.DMA],
        ),
        out_shape=jax.ShapeDtypeStruct((t, d), F32),
        compiler_params=pltpu.CompilerParams(
            dimension_semantics=("arbitrary",), vmem_limit_bytes=VMEM_LIMIT),
        name="moe_combine",
    )(pos, h, ys, final_gain.reshape(1, d))


def grouped_moe(h, xe, grp, rank, cnt, w1, w3, w2, final_gain, final_norm, rt=512):
    t = h.shape[0]
    rt = min(rt, t)
    pos, tile_group, tile_valid = _route_tables(cnt, grp, rank, t, rt)
    xs = moe_dispatch(pos, xe, t + N_GROUPS * rt)
    ys = moe_ffn(tile_group, tile_valid, xs, w1, w3, w2, rt)
    return moe_combine(pos, h, ys, final_gain, final_norm)


def kernel(x, norm_mix_gain, w_in, gmlp_w_s, gmlp_b_s, gmlp_v_gain, diff_lambda, diff_subln_gain,
           w_up_a, w_up_b, w_up_c, w_out, norm_ffn_gain, router_w, router_bias,
           moe_w1, moe_w3, moe_w2, final_gain):
    bsz, seq, d = x.shape
    depth = w_in.shape[0]
    x2 = x.reshape(bsz * seq, d)
    router_wt = router_w.T
    for l in range(depth):
        lambda_init = 0.8 - 0.6 * math.exp(-0.3 * l)
        proj = in_proj(x2, norm_mix_gain[l], w_in[l].astype(BF16))
        yb = diff_attn(proj, diff_lambda[l], diff_subln_gain[l], bsz, seq, lambda_init)
        yc = stick_attn(proj, bsz, seq)
        h, xe, grp, rank, cnt = merge(x2, proj, yb, yc, gmlp_w_s[l], gmlp_b_s[l].T, gmlp_v_gain[l],
                                      w_up_a[l].astype(BF16), w_up_b[l].astype(BF16), w_up_c[l].astype(BF16),
                                      w_out[l].astype(BF16), norm_ffn_gain[l], router_wt, router_bias)
        x2 = grouped_moe(h, xe, grp, rank, cnt, moe_w1[l].astype(BF16), moe_w3[l].astype(BF16),
                         moe_w2[l].astype(BF16), final_gain, final_norm=(l == depth - 1))
    return x2.reshape(bsz, seq, d)
```

```python
import functools
import math

import jax
import jax.numpy as jnp
from jax import lax
from jax.experimental import pallas as pl
from jax.experimental.pallas import tpu as pltpu

D_MODEL = 1024
BLOCK = 128
A_GROUPS = 4
A_WIDTH = 512
B_HEADS = 4
B_QK_DIM = 64
B_V_DIM = 128
C_HEADS = 8
C_HEAD_DIM = 64
C_WIDTH = 512
N_EXPERTS = 16
N_GROUPS = 4
EXPERTS_PER_GROUP = 4
D_FF_EXPERT = 512
IN_WIDTH = 7168
XE_WIDTH = D_MODEL + 128
EPS = 1e-6

LANES = 128
VMEM_LIMIT = 56 * 1024 * 1024
F32 = jnp.float32
BF16 = jnp.bfloat16
NEG_BIG = -1e30
EXP_UNDERFLOW = -104.0

LOG2E = 1.4426950408889634

COL_QB, COL_KB, COL_VB = 8, 12, 16
QB_COLS = (2 * A_WIDTH, 2 * A_WIDTH + 512)
QC_COLS = (2 * A_WIDTH + 3 * 512, 2 * A_WIDTH + 4 * 512)


def _nt_dot(a, b):
    return lax.dot_general(a, b, (((1,), (1,)), ((), ())), preferred_element_type=F32)


def _dot(a, b):
    return jnp.dot(a, b, preferred_element_type=F32)


def _rms(xf, gain):
    return xf * lax.rsqrt(jnp.mean(xf * xf, axis=-1, keepdims=True) + EPS) * gain


def _in_proj_kernel(x_ref, g_ref, w_ref, cs_ref, o_ref, xn_ref):
    @pl.when(pl.program_id(1) == 0)
    def _():
        xn_ref[...] = _rms(x_ref[...], g_ref[...]).astype(BF16)

    o_ref[...] = (_dot(xn_ref[...], w_ref[...]) * cs_ref[...]).astype(BF16)


def _query_scale_row(n):
    col = jnp.arange(n)
    s = jnp.ones((n,), F32)
    s = jnp.where((col >= QB_COLS[0]) & (col < QB_COLS[1]), LOG2E * B_QK_DIM ** -0.5, s)
    s = jnp.where((col >= QC_COLS[0]) & (col < QC_COLS[1]), C_HEAD_DIM ** -0.5, s)
    return s.reshape(1, n)


def in_proj(x2, gain, w_bf16, tm=1024, tn=1024):
    t, d = x2.shape
    n = w_bf16.shape[1]
    tm = min(tm, t)
    return pl.pallas_call(
        _in_proj_kernel,
        grid=(t // tm, n // tn),
        in_specs=[
            pl.BlockSpec((tm, d), lambda i, j: (i, 0)),
            pl.BlockSpec((1, d), lambda i, j: (0, 0)),
            pl.BlockSpec((d, tn), lambda i, j: (0, j)),
            pl.BlockSpec((1, tn), lambda i, j: (0, j)),
        ],
        out_specs=pl.BlockSpec((tm, tn), lambda i, j: (i, j)),
        out_shape=jax.ShapeDtypeStruct((t, n), BF16),
        scratch_shapes=[pltpu.VMEM((tm, d), BF16)],
        compiler_params=pltpu.CompilerParams(
            dimension_semantics=("arbitrary", "arbitrary"), vmem_limit_bytes=VMEM_LIMIT),
        name="in_proj",
    )(x2, gain.reshape(1, d), w_bf16, _query_scale_row(n))


def _diff_attn_kernel(lam_ref, sg_ref, q_ref, k_ref, v_ref, o_ref, m_ref, l_ref, acc_ref, s_ref, p_ref,
                      *, tq, lambda_init):
    tk = tq
    h = pl.program_id(1)
    qi = pl.program_id(2)
    nc = tk // LANES
    slope = jnp.where(h == 0, 0.25, jnp.where(h == 1, 0.0625, jnp.where(h == 2, 0.015625, 0.00390625)))
    slope = slope.astype(F32) * LOG2E

    lane = lax.broadcasted_iota(jnp.int32, (tq, LANES), 1)
    q = q_ref[...]
    zero = jnp.zeros_like(q)
    qs = (jnp.where(lane < B_QK_DIM, q, zero), jnp.where(lane >= B_QK_DIM, q, zero))

    m_ref[...] = jnp.full(m_ref.shape, NEG_BIG, F32)
    l_ref[...] = jnp.zeros(l_ref.shape, F32)
    acc_ref[...] = jnp.zeros(acc_ref.shape, F32)

    kcol = lax.broadcasted_iota(jnp.int32, (1, LANES), 1)
    row_minus_col = lax.broadcasted_iota(jnp.int32, (tq, LANES), 0) - lane
    q0 = qi * tq

    def load(ref, kb):
        return ref[pl.ds(pl.multiple_of(kb * tk, tk), tk), :]

    def scores(kb):
        kk = load(k_ref, kb)
        return [_nt_dot(qs[mi], kk) for mi in range(2)]

    def softmax_step(mi, s, kb, masked):
        off = kb * tk - q0
        cols = []
        for c in range(nc):
            sc = s[:, c * LANES:(c + 1) * LANES] + slope * (kcol + (off + c * LANES)).astype(F32)
            if masked:
                sc = jnp.where(row_minus_col >= (off + c * LANES), sc, NEG_BIG)
            cols.append(sc)
        mx = cols[0]
        for sc in cols[1:]:
            mx = jnp.maximum(mx, sc)
        m_prev = m_ref[mi]
        m_new = jnp.maximum(m_prev, jnp.max(mx, axis=-1, keepdims=True))
        alpha = jnp.exp2(m_prev - m_new)
        ps = [jnp.exp2(sc - m_new) for sc in cols]
        psum = ps[0]
        for pc in ps[1:]:
            psum = psum + pc
        l_ref[mi] = alpha * l_ref[mi] + psum
        acc_ref[mi] = alpha * acc_ref[mi]
        m_ref[mi] = m_new
        return jnp.concatenate([pc.astype(BF16) for pc in ps], axis=1)

    odd = lax.rem(qi, 2)
    s0 = scores(0)
    for mi in range(2):
        s_ref[odd, mi] = s0[mi]
        p_ref[odd, mi] = jnp.zeros((tq, tk), BF16)

    def pending_pv(j, slot):
        vprev = load(v_ref, jnp.maximum(j - 1, 0))
        return [_dot(p_ref[slot, mi], vprev) for mi in range(2)]

    def step(j, slot):
        nxt = 1 - slot
        pv = pending_pv(j, slot)
        sn = scores(j + 1)
        for mi in range(2):
            acc_ref[mi] += pv[mi]
            s_ref[nxt, mi] = sn[mi]
        for mi in range(2):
            p_ref[nxt, mi] = softmax_step(mi, s_ref[slot, mi], j, False)

    @pl.when(odd == 1)
    def _():
        step(0, 1)

    def body(t, c):
        j = odd + 2 * t
        step(j, 0)
        step(j + 1, 1)
        return c

    lax.fori_loop(0, qi // 2, body, 0)
    pv = pending_pv(qi, 0)
    for mi in range(2):
        acc_ref[mi] += pv[mi]
    vdiag = load(v_ref, qi)
    for mi in range(2):
        p = softmax_step(mi, s_ref[0, mi], qi, True)
        acc_ref[mi] += _dot(p, vdiag)

    lp = lam_ref[...]
    s1 = jnp.sum(lp[0:1] * lp[1:2], axis=-1, keepdims=True)
    s2 = jnp.sum(lp[2:3] * lp[3:4], axis=-1, keepdims=True)
    lam = jnp.exp(s1) - jnp.exp(s2) + lambda_init
    l0 = jnp.sum(l_ref[0], axis=-1, keepdims=True)
    l1 = jnp.sum(l_ref[1], axis=-1, keepdims=True)
    o = acc_ref[0] / l0 - lam * (acc_ref[1] / l1)
    o_ref[...] = (_rms(o, sg_ref[...]) * (1.0 - lambda_init)).astype(BF16)


def diff_attn(proj, lam_params, subln_g, bsz, seq, lambda_init, tq=512):
    tq = min(tq, seq)
    nq = seq // tq
    kern = functools.partial(_diff_attn_kernel, tq=tq, lambda_init=lambda_init)
    return pl.pallas_call(
        kern,
        grid=(bsz, B_HEADS, nq),
        in_specs=[
            pl.BlockSpec((4, B_QK_DIM), lambda b, h, i: (0, 0)),
            pl.BlockSpec((1, B_V_DIM), lambda b, h, i: (0, 0)),
            pl.BlockSpec((tq, LANES), lambda b, h, i: (b * nq + i, COL_QB + h)),
            pl.BlockSpec((seq, LANES), lambda b, h, i: (b, COL_KB + h)),
            pl.BlockSpec((seq, LANES), lambda b, h, i: (b, COL_VB + h)),
        ],
        out_specs=pl.BlockSpec((tq, LANES), lambda b, h, i: (b * nq + i, h)),
        out_shape=jax.ShapeDtypeStruct((bsz * seq, B_HEADS * B_V_DIM), BF16),
        scratch_shapes=[pltpu.VMEM((2, tq, LANES), F32), pltpu.VMEM((2, tq, LANES), F32),
                        pltpu.VMEM((2, tq, B_V_DIM), F32),
                        pltpu.VMEM((2, 2, tq, tq), F32), pltpu.VMEM((2, 2, tq, tq), BF16)],
        compiler_params=pltpu.CompilerParams(
            dimension_semantics=("arbitrary", "arbitrary", "arbitrary"), vmem_limit_bytes=VMEM_LIMIT),
        name="diff_attn",
    )(lam_params, subln_g.reshape(1, B_V_DIM), proj, proj, proj)


def _stick_kernel(q_ref, k_ref, v_ref, o_ref, qs_ref, carry_ref, acc_ref, *, tq):
    tk = tq
    n_pairs = C_HEADS // 2
    qi = pl.program_id(1)
    lane = lax.broadcasted_iota(jnp.int32, (tq, LANES), 1)
    low = lane < C_HEAD_DIM
    for p in range(n_pairs):
        q = q_ref[:, p * LANES:(p + 1) * LANES]
        zero = jnp.zeros_like(q)
        qs_ref[p, 0:tq] = jnp.where(low, q, zero)
        qs_ref[p, tq:2 * tq] = jnp.where(low, zero, q)
    carry_ref[...] = jnp.zeros(carry_ref.shape, F32)
    acc_ref[...] = jnp.zeros(acc_ref.shape, F32)

    row = lax.broadcasted_iota(jnp.int32, (2 * tq, tk), 0)
    col = lax.broadcasted_iota(jnp.int32, (2 * tq, tk), 1)
    strict = (row % tq) > col
    uj = lax.broadcasted_iota(jnp.int32, (2 * tk, tk + LANES), 0) % tk
    us = lax.broadcasted_iota(jnp.int32, (2 * tk, tk + LANES), 1)
    w2 = jnp.where(jnp.logical_or(uj > us, us >= tk), 1.0, 0.0).astype(BF16)
    vlow = lax.broadcasted_iota(jnp.int32, (tk, LANES), 1) < C_HEAD_DIM

    def step(kb, masked):
        start = pl.multiple_of(kb * tk, tk)
        pairs = range(n_pairs)
        cols = [slice(p * LANES, (p + 1) * LANES) for p in pairs]
        zs = [_nt_dot(qs_ref[p], k_ref[pl.ds(start, tk), cols[p]]) for p in pairs]
        lbs, rs = [], []
        for p in pairs:
            z = zs[p]
            sp = jnp.log(1.0 + jnp.exp(-jnp.abs(z)))
            lb = jnp.minimum(z, 0.0) - sp
            lk = lb - z
            if masked:
                lk = jnp.where(strict, lk, 0.0)
            hi = lk.astype(BF16)
            lo = (lk - hi.astype(F32)).astype(BF16)
            lbs.append(lb)
            rs.append(_dot(jnp.concatenate([hi, lo], axis=1), w2))
        mx = None
        for p in pairs:
            carry = carry_ref[p]
            a = jnp.exp(lbs[p] + rs[p][:, :tk] + carry)
            if masked:
                a = jnp.where(strict, a, 0.0)
            carry = carry + rs[p][:, tk:]
            carry_ref[p] = carry
            mx = carry if mx is None else jnp.maximum(mx, carry)
            ab = a.astype(BF16)
            vv = v_ref[pl.ds(start, tk), cols[p]]
            zero = jnp.zeros_like(vv)
            vst = jnp.concatenate([jnp.where(vlow, vv, zero), jnp.where(vlow, zero, vv)], axis=0)
            acc_ref[p] += _dot(jnp.concatenate([ab[:tq], ab[tq:]], axis=1), vst)
        return jnp.max(mx)

    def cond(st):
        kb, alive = st
        return jnp.logical_and(kb >= 0, alive > EXP_UNDERFLOW)

    def body(st):
        kb, _ = st
        return kb - 1, step(kb, False)

    lax.while_loop(cond, body, (qi - 1, step(qi, True)))
    for p in range(n_pairs):
        o_ref[:, p * LANES:(p + 1) * LANES] = acc_ref[p].astype(BF16)


def stick_attn(proj, bsz, seq, tq=128):
    nq = seq // tq
    kern = functools.partial(_stick_kernel, tq=tq)
    n_pairs = C_HEADS // 2
    return pl.pallas_call(
        kern,
        grid=(bsz, nq),
        in_specs=[
            pl.BlockSpec((tq, C_WIDTH), lambda b, i: (b * nq + i, 5)),
            pl.BlockSpec((seq, C_WIDTH), lambda b, i: (b, 6)),
            pl.BlockSpec((seq, C_WIDTH), lambda b, i: (b, 7)),
        ],
        out_specs=pl.BlockSpec((tq, C_WIDTH), lambda b, i: (b * nq + i, 0)),
        out_shape=jax.ShapeDtypeStruct((bsz * seq, C_WIDTH), BF16),
        scratch_shapes=[pltpu.VMEM((n_pairs, 2 * tq, LANES), BF16),
                        pltpu.VMEM((n_pairs, 2 * tq, LANES), F32),
                        pltpu.VMEM((n_pairs, tq, LANES), F32)],
        compiler_params=pltpu.CompilerParams(
            dimension_semantics=("arbitrary", "arbitrary"), vmem_limit_bytes=VMEM_LIMIT),
        name="stick_attn",
    )(proj, proj, proj)


def _top2_route(sel, aff):
    def top2_sum(a, b, c, d):
        hi1, lo1 = jnp.maximum(a, b), jnp.minimum(a, b)
        hi2, lo2 = jnp.maximum(c, d), jnp.minimum(c, d)
        return jnp.maximum(hi1, hi2) + jnp.maximum(jnp.minimum(hi1, hi2), jnp.maximum(lo1, lo2))

    scores = [top2_sum(*sel[4 * g:4 * g + 4]) for g in range(N_GROUPS)]
    best = jnp.zeros_like(scores[0], dtype=jnp.int32)
    best_score = scores[0]
    for g in range(1, N_GROUPS):
        better = scores[g] > best_score
        best = jnp.where(better, g, best)
        best_score = jnp.where(better, scores[g], best_score)

    def pick(rows, i):
        out = rows[i]
        for g in range(1, N_GROUPS):
            out = jnp.where(best == g, rows[4 * g + i], out)
        return out

    cs = [pick(sel, i) for i in range(EXPERTS_PER_GROUP)]
    ca = [pick(aff, i) for i in range(EXPERTS_PER_GROUP)]

    def argmax_first(vals, exclude=None):
        idx = jnp.full(vals[0].shape, -1, jnp.int32)
        cur = jnp.full(vals[0].shape, -jnp.inf, F32)
        for i, v in enumerate(vals):
            ok = v > cur
            if exclude is not None:
                ok = jnp.logical_and(ok, exclude != i)
            idx = jnp.where(ok, i, idx)
            cur = jnp.where(ok, v, cur)
        return idx

    i1 = argmax_first(cs)
    i2 = argmax_first(cs, exclude=i1)

    def take(vals, idx):
        out = vals[0]
        for i in range(1, len(vals)):
            out = jnp.where(idx == i, vals[i], out)
        return out

    w1, w2 = take(ca, i1), take(ca, i2)
    tot = w1 + w2
    w1, w2 = w1 / tot, w2 / tot
    e1 = best * EXPERTS_PER_GROUP + i1
    e2 = best * EXPERTS_PER_GROUP + i2
    gates = [jnp.where(e1 == e, w1, 0.0) + jnp.where(e2 == e, w2, 0.0) for e in range(N_EXPERTS)]
    return gates, best


def _merge_kernel(x_ref, u_ref, v_ref, g0_ref, g1_ref, g2_ref, yb_ref, yc_ref,
                  ws_ref, bst_ref, gv_ref, wa_ref, wb_ref, wc_ref, wo_ref, ng_ref, rwt_ref, rb_ref,
                  h_ref, xe_ref, grp_ref, rank_ref, cnt_ref, ya_ref, run_ref, *, tm):
    u = jax.nn.gelu(u_ref[...].astype(F32))
    v = _rms(jax.nn.gelu(v_ref[...].astype(F32)), gv_ref[...]).astype(BF16)

    r = lax.broadcasted_iota(jnp.int32, (BLOCK, BLOCK), 0)
    c = lax.broadcasted_iota(jnp.int32, (BLOCK, BLOCK), 1)
    causal = r >= c
    for g in range(A_GROUPS):
        w = jnp.where(causal, ws_ref[g], 0.0).astype(BF16)
        bias = bst_ref[:, g:g + 1]
        cols = slice(g * LANES, (g + 1) * LANES)
        for ch in range(tm // BLOCK):
            rows = slice(ch * BLOCK, (ch + 1) * BLOCK)
            mixed = _dot(w, v[rows, cols]) + bias
            ya_ref[rows, cols] = (u[rows, cols] * mixed).astype(BF16)

    merged = jax.nn.sigmoid(g0_ref[...].astype(F32)) * _dot(ya_ref[...], wa_ref[...])
    merged += jax.nn.sigmoid(g1_ref[...].astype(F32)) * _dot(yb_ref[...], wb_ref[...])
    merged += jax.nn.sigmoid(g2_ref[...].astype(F32)) * _dot(yc_ref[...], wc_ref[...])
    h = x_ref[...] + _dot(merged.astype(BF16), wo_ref[...])
    h_ref[...] = h

    xn = _rms(h, ng_ref[...])
    xe_ref[:, :D_MODEL] = xn

    xh = xn.astype(BF16)
    xl = (xn - xh.astype(F32)).astype(BF16)
    rw = rwt_ref[...]
    rh = rw.astype(BF16)
    rl = (rw - rh.astype(F32)).astype(BF16)
    logits = _nt_dot(rh, xh) + (_nt_dot(rh, xl) + _nt_dot(rl, xh))
    aff = jax.nn.sigmoid(logits)
    sel = aff + rb_ref[...]
    gates, best = _top2_route([sel[e:e + 1] for e in range(N_EXPERTS)],
                              [aff[e:e + 1] for e in range(N_EXPERTS)])
    gate_rows = jnp.concatenate(gates + [jnp.zeros((LANES - N_EXPERTS, tm), F32)], axis=0)
    xe_ref[:, D_MODEL:] = gate_rows.T

    @pl.when(pl.program_id(0) == 0)
    def _():
        run_ref[...] = jnp.zeros(run_ref.shape, F32)

    gid = lax.broadcasted_iota(jnp.int32, (8, tm), 0)
    member = (gid == best).astype(F32)
    before = lax.broadcasted_iota(jnp.int32, (tm, tm), 0) < lax.broadcasted_iota(jnp.int32, (tm, tm), 1)
    prefix = _dot(member.astype(BF16), before.astype(BF16))
    run = run_ref[...]
    rank = jnp.sum(member * (prefix + run[:, 0:1]), axis=0, keepdims=True)
    rank_ref[...] = rank.astype(jnp.int32)
    grp_ref[...] = best
    run = run + jnp.sum(member, axis=-1, keepdims=True)
    run_ref[...] = run
    cnt_ref[...] = run


def merge(x2, proj, yb, yc, w_s, b_s_t, g_v, wa, wb, wc, wo, norm_g, router_wt, router_b, tm=512):
    t, d = x2.shape
    tm = min(tm, t)
    row = lambda i: (i, 0)
    const2 = lambda i: (0, 0)
    kern = functools.partial(_merge_kernel, tm=tm)
    return pl.pallas_call(
        kern,
        grid=(t // tm,),
        in_specs=[
            pl.BlockSpec((tm, d), row),
            pl.BlockSpec((tm, A_WIDTH), lambda i: (i, 0)),
            pl.BlockSpec((tm, A_WIDTH), lambda i: (i, 1)),
            pl.BlockSpec((tm, d), lambda i: (i, 4)),
            pl.BlockSpec((tm, d), lambda i: (i, 5)),
            pl.BlockSpec((tm, d), lambda i: (i, 6)),
            pl.BlockSpec((tm, A_WIDTH), row),
            pl.BlockSpec((tm, C_WIDTH), row),
            pl.BlockSpec((A_GROUPS, BLOCK, BLOCK), lambda i: (0, 0, 0)),
            pl.BlockSpec((BLOCK, A_GROUPS), const2),
            pl.BlockSpec((1, A_WIDTH), const2),
            pl.BlockSpec((A_WIDTH, d), const2),
            pl.BlockSpec((A_WIDTH, d), const2),
            pl.BlockSpec((C_WIDTH, d), const2),
            pl.BlockSpec((d, d), const2),
            pl.BlockSpec((1, d), const2),
            pl.BlockSpec((N_EXPERTS, d), const2),
            pl.BlockSpec((N_EXPERTS, 1), const2),
        ],
        out_specs=[
            pl.BlockSpec((tm, d), row),
            pl.BlockSpec((tm, XE_WIDTH), row),
            pl.BlockSpec((1, tm), lambda i: (0, i)),
            pl.BlockSpec((1, tm), lambda i: (0, i)),
            pl.BlockSpec((8, LANES), const2),
        ],
        out_shape=[
            jax.ShapeDtypeStruct((t, d), F32),
            jax.ShapeDtypeStruct((t, XE_WIDTH), F32),
            jax.ShapeDtypeStruct((1, t), jnp.int32),
            jax.ShapeDtypeStruct((1, t), jnp.int32),
            jax.ShapeDtypeStruct((8, LANES), F32),
        ],
        scratch_shapes=[pltpu.VMEM((tm, A_WIDTH), BF16), pltpu.VMEM((8, LANES), F32)],
        compiler_params=pltpu.CompilerParams(
            dimension_semantics=("arbitrary",), vmem_limit_bytes=VMEM_LIMIT),
        name="merge",
    )(x2, proj, proj, proj, proj, proj, yb, yc, w_s, b_s_t, g_v.reshape(1, A_WIDTH),
      wa, wb, wc, wo, norm_g.reshape(1, d), router_wt, router_b.reshape(N_EXPERTS, 1))


def _route_tables(cnt, grp, rank, t, rt):
    cnt = cnt[:N_GROUPS, 0].astype(jnp.int32)
    padded = (cnt + rt - 1) // rt * rt
    off_end = jnp.cumsum(padded)
    off = off_end - padded
    g = grp.reshape(t)
    pos = rank.reshape(t)
    for k in range(N_GROUPS):
        pos = pos + jnp.where(g == k, off[k], 0)
    starts = jnp.arange(t // rt + N_GROUPS, dtype=jnp.int32) * rt
    tile_group = jnp.minimum(jnp.sum((starts[:, None] >= off_end[None, :]).astype(jnp.int32), axis=1),
                             N_GROUPS - 1)
    tile_valid = (starts < off_end[N_GROUPS - 1]).astype(jnp.int32)
    return pos, tile_group, tile_valid


ROW_DMA_UNROLL = 8


def _row_copy(src_ref, src_row, dst_ref, dst_row, sem):
    return pltpu.make_async_copy(src_ref.at[pl.ds(src_row, 1)], dst_ref.at[pl.ds(dst_row, 1)], sem)


def _dispatch_kernel(pos_ref, xe_ref, xs_init_ref, xs_ref, sem, *, tm):
    del xs_init_ref
    base = pl.program_id(0) * tm

    def issue(r, c):
        _row_copy(xe_ref, r, xs_ref, pos_ref[base + r], sem).start()
        return c

    lax.fori_loop(0, tm, issue, 0, unroll=ROW_DMA_UNROLL)
    pltpu.make_async_copy(xe_ref, xs_ref.at[pl.ds(0, tm)], sem).wait()


def moe_dispatch(pos, xe, t_pad, tm=512):
    t, w = xe.shape
    tm = min(tm, t)
    kern = functools.partial(_dispatch_kernel, tm=tm)
    return pl.pallas_call(
        kern,
        grid_spec=pltpu.PrefetchScalarGridSpec(
            num_scalar_prefetch=1,
            grid=(t // tm,),
            in_specs=[pl.BlockSpec((tm, w), lambda i, pos: (i, 0)),
                      pl.BlockSpec(memory_space=pl.ANY)],
            out_specs=pl.BlockSpec(memory_space=pl.ANY),
            scratch_shapes=[pltpu.SemaphoreType.DMA],
        ),
        out_shape=jax.ShapeDtypeStruct((t_pad, w), F32),
        input_output_aliases={2: 0},
        compiler_params=pltpu.CompilerParams(
            dimension_semantics=("arbitrary",), vmem_limit_bytes=VMEM_LIMIT),
        name="moe_dispatch",
    )(pos, xe, jnp.zeros((t_pad, w), F32))


def _moe_ffn_kernel(tg_ref, tv_ref, xs_ref, w1_ref, w3_ref, w2_ref, ys_ref):
    i = pl.program_id(0)
    valid = tv_ref[i] == 1

    @pl.when(valid)
    def _():
        x = xs_ref[:, :D_MODEL].astype(BF16)
        ext = xs_ref[:, D_MODEL:]
        lane = lax.broadcasted_iota(jnp.int32, ext.shape, 1)
        first = tg_ref[i] * EXPERTS_PER_GROUP

        def up(e):
            return _dot(x, w1_ref[e]), _dot(x, w3_ref[e])

        def down(e, ab):
            gcol = jnp.sum(jnp.where(lane == first + e, ext, 0.0), axis=-1, keepdims=True)
            hh = ((jax.nn.silu(ab[0]) * ab[1]) * gcol).astype(BF16)
            return _dot(hh, w2_ref[e])

        ab = up(0)
        out = None
        for e in range(EXPERTS_PER_GROUP):
            nxt = up(e + 1) if e + 1 < EXPERTS_PER_GROUP else None
            o = down(e, ab)
            out = o if out is None else out + o
            ab = nxt
        ys_ref[...] = out

    @pl.when(jnp.logical_not(valid))
    def _():
        ys_ref[...] = jnp.zeros(ys_ref.shape, F32)


def moe_ffn(tile_group, tile_valid, xs, w1, w3, w2, rt):
    t_pad, w = xs.shape
    d = D_MODEL
    w1, w3, w2 = (a.reshape(N_GROUPS, EXPERTS_PER_GROUP, *a.shape[1:]) for a in (w1, w3, w2))
    widx = lambda i, tg, tv: (tg[i], 0, 0, 0)
    return pl.pallas_call(
        _moe_ffn_kernel,
        grid_spec=pltpu.PrefetchScalarGridSpec(
            num_scalar_prefetch=2,
            grid=(t_pad // rt,),
            in_specs=[pl.BlockSpec((rt, w), lambda i, tg, tv: (i, 0)),
                      pl.BlockSpec((None, EXPERTS_PER_GROUP, d, D_FF_EXPERT), widx),
                      pl.BlockSpec((None, EXPERTS_PER_GROUP, d, D_FF_EXPERT), widx),
                      pl.BlockSpec((None, EXPERTS_PER_GROUP, D_FF_EXPERT, d), widx)],
            out_specs=pl.BlockSpec((rt, d), lambda i, tg, tv: (i, 0)),
        ),
        out_shape=jax.ShapeDtypeStruct((t_pad, d), F32),
        compiler_params=pltpu.CompilerParams(
            dimension_semantics=("arbitrary",), vmem_limit_bytes=VMEM_LIMIT),
        name="moe_ffn",
    )(tile_group, tile_valid, xs, w1, w3, w2)


def _combine_kernel(pos_ref, h_ref, ys_ref, fg_ref, o_ref, ybuf_ref, sem, *, tm, final_norm):
    base = pl.program_id(0) * tm

    def issue(r, c):
        _row_copy(ys_ref, pos_ref[base + r], ybuf_ref, r, sem).start()
        return c

    lax.fori_loop(0, tm, issue, 0, unroll=ROW_DMA_UNROLL)
    pltpu.make_async_copy(ys_ref.at[pl.ds(0, tm)], ybuf_ref, sem).wait()
    y = h_ref[...] + ybuf_ref[...]
    if final_norm:
        y = _rms(y, fg_ref[...])
    o_ref[...] = y


def moe_combine(pos, h, ys, final_gain, final_norm, tm=512):
    t, d = h.shape
    tm = min(tm, t)
    kern = functools.partial(_combine_kernel, tm=tm, final_norm=final_norm)
    return pl.pallas_call(
        kern,
        grid_spec=pltpu.PrefetchScalarGridSpec(
            num_scalar_prefetch=1,
            grid=(t // tm,),
            in_specs=[pl.BlockSpec((tm, d), lambda i, pos: (i, 0)),
                      pl.BlockSpec(memory_space=pl.ANY),
                      pl.BlockSpec((1, d), lambda i, pos: (0, 0))],
            out_specs=pl.BlockSpec((tm, d), lambda i, pos: (i, 0)),
            scratch_shapes=[pltpu.VMEM((tm, d), F32), pltpu.SemaphoreType.DMA],
        ),
        out_shape=jax.ShapeDtypeStruct((t, d), F32),
        compiler_params=pltpu.CompilerParams(
            dimension_semantics=("arbitrary",), vmem_limit_bytes=VMEM_LIMIT),
        name="moe_combine",
    )(pos, h, ys, final_gain.reshape(1, d))


def grouped_moe(h, xe, grp, rank, cnt, w1, w3, w2, final_gain, final_norm, rt=512):
    t = h.shape[0]
    rt = min(rt, t)
    pos, tile_group, tile_valid = _route_tables(cnt, grp, rank, t, rt)
    xs = moe_dispatch(pos, xe, t + N_GROUPS * rt)
    ys = moe_ffn(tile_group, tile_valid, xs, w1, w3, w2, rt)
    return moe_combine(pos, h, ys, final_gain, final_norm)


def kernel(x, norm_mix_gain, w_in, gmlp_w_s, gmlp_b_s, gmlp_v_gain, diff_lambda, diff_subln_gain,
           w_up_a, w_up_b, w_up_c, w_out, norm_ffn_gain, router_w, router_bias,
           moe_w1, moe_w3, moe_w2, final_gain):
    bsz, seq, d = x.shape
    depth = w_in.shape[0]
    x2 = x.reshape(bsz * seq, d)
    router_wt = router_w.T
    for l in range(depth):
        lambda_init = 0.8 - 0.6 * math.exp(-0.3 * l)
        proj = in_proj(x2, norm_mix_gain[l], w_in[l].astype(BF16))
        yb = diff_attn(proj, diff_lambda[l], diff_subln_gain[l], bsz, seq, lambda_init)
        yc = stick_attn(proj, bsz, seq)
        h, xe, grp, rank, cnt = merge(x2, proj, yb, yc, gmlp_w_s[l], gmlp_b_s[l].T, gmlp_v_gain[l],
                                      w_up_a[l].astype(BF16), w_up_b[l].astype(BF16), w_up_c[l].astype(BF16),
                                      w_out[l].astype(BF16), norm_ffn_gain[l], router_wt, router_bias)
        x2 = grouped_moe(h, xe, grp, rank, cnt, moe_w1[l].astype(BF16), moe_w3[l].astype(BF16),
                         moe_w2[l].astype(BF16), final_gain, final_norm=(l == depth - 1))
    return x2.reshape(bsz, seq, d)
```

```python
import functools
import math

import jax
import jax.numpy as jnp
from jax import lax
from jax.experimental import pallas as pl
from jax.experimental.pallas import tpu as pltpu

D_MODEL = 1024
BLOCK = 128
A_GROUPS = 4
A_WIDTH = 512
B_HEADS = 4
B_QK_DIM = 64
B_V_DIM = 128
C_HEADS = 8
C_HEAD_DIM = 64
C_WIDTH = 512
N_EXPERTS = 16
N_GROUPS = 4
EXPERTS_PER_GROUP = 4
D_FF_EXPERT = 512
IN_WIDTH = 7168
XE_WIDTH = D_MODEL + 128
EPS = 1e-6

LANES = 128
VMEM_LIMIT = 56 * 1024 * 1024
F32 = jnp.float32
BF16 = jnp.bfloat16
NEG_BIG = -1e30
EXP_UNDERFLOW = -104.0

LOG2E = 1.4426950408889634

COL_QB, COL_KB, COL_VB = 8, 12, 16
QB_COLS = (2 * A_WIDTH, 2 * A_WIDTH + 512)
QC_COLS = (2 * A_WIDTH + 3 * 512, 2 * A_WIDTH + 4 * 512)


def _nt_dot(a, b):
    return lax.dot_general(a, b, (((1,), (1,)), ((), ())), preferred_element_type=F32)


def _dot(a, b):
    return jnp.dot(a, b, preferred_element_type=F32)


def _sigmoid(x):
    return 0.5 * jnp.tanh(0.5 * x) + 0.5


def _rms(xf, gain):
    return xf * lax.rsqrt(jnp.mean(xf * xf, axis=-1, keepdims=True) + EPS) * gain


def _in_proj_kernel(x_ref, g_ref, w_ref, cs_ref, o_ref, xn_ref):
    @pl.when(pl.program_id(1) == 0)
    def _():
        xn_ref[...] = _rms(x_ref[...], g_ref[...]).astype(BF16)

    o_ref[...] = (_dot(xn_ref[...], w_ref[...]) * cs_ref[...]).astype(BF16)


def _query_scale_row(n):
    col = jnp.arange(n)
    s = jnp.ones((n,), F32)
    s = jnp.where((col >= QB_COLS[0]) & (col < QB_COLS[1]), LOG2E * B_QK_DIM ** -0.5, s)
    s = jnp.where((col >= QC_COLS[0]) & (col < QC_COLS[1]), C_HEAD_DIM ** -0.5, s)
    return s.reshape(1, n)


def in_proj(x2, gain, w_bf16, tm=1024, tn=1792):
    t, d = x2.shape
    n = w_bf16.shape[1]
    tm = min(tm, t)
    return pl.pallas_call(
        _in_proj_kernel,
        grid=(t // tm, n // tn),
        in_specs=[
            pl.BlockSpec((tm, d), lambda i, j: (i, 0)),
            pl.BlockSpec((1, d), lambda i, j: (0, 0)),
            pl.BlockSpec((d, tn), lambda i, j: (0, j)),
            pl.BlockSpec((1, tn), lambda i, j: (0, j)),
        ],
        out_specs=pl.BlockSpec((tm, tn), lambda i, j: (i, j)),
        out_shape=jax.ShapeDtypeStruct((t, n), BF16),
        scratch_shapes=[pltpu.VMEM((tm, d), BF16)],
        compiler_params=pltpu.CompilerParams(
            dimension_semantics=("arbitrary", "arbitrary"), vmem_limit_bytes=VMEM_LIMIT),
        name="in_proj",
    )(x2, gain.reshape(1, d), w_bf16, _query_scale_row(n))


def _diff_attn_kernel(lam_ref, sg_ref, q_ref, k_ref, v_ref, o_ref, m_ref, l_ref, acc_ref, s_ref, p_ref,
                      *, tq, hps, lambda_init):
    tk = tq
    hp = pl.program_id(1)
    qi = pl.program_id(2)
    nc = tk // LANES
    units = [(hh, mi) for hh in range(hps) for mi in range(2)]
    head_cols = [slice(hh * LANES, (hh + 1) * LANES) for hh in range(hps)]

    def head_slope(h):
        sl = jnp.where(h == 0, 0.25, jnp.where(h == 1, 0.0625, jnp.where(h == 2, 0.015625, 0.00390625)))
        return sl.astype(F32) * LOG2E

    slopes = [head_slope(hp * hps + hh) for hh in range(hps)]

    lane = lax.broadcasted_iota(jnp.int32, (tq, LANES), 1)
    qs = []
    for hh, mi in units:
        q = q_ref[:, head_cols[hh]]
        keep = (lane < B_QK_DIM) if mi == 0 else (lane >= B_QK_DIM)
        qs.append(jnp.where(keep, q, jnp.zeros_like(q)))

    m_ref[...] = jnp.full(m_ref.shape, NEG_BIG, F32)
    l_ref[...] = jnp.zeros(l_ref.shape, F32)
    acc_ref[...] = jnp.zeros(acc_ref.shape, F32)

    kcol = lax.broadcasted_iota(jnp.int32, (1, LANES), 1)
    row_minus_col = lax.broadcasted_iota(jnp.int32, (tq, LANES), 0) - lane
    q0 = qi * tq

    def load(ref, kb, hh):
        return ref[pl.ds(pl.multiple_of(kb * tk, tk), tk), head_cols[hh]]

    def scores(kb):
        kks = [load(k_ref, kb, hh) for hh in range(hps)]
        return [_nt_dot(qs[u], kks[hh]) for u, (hh, _) in enumerate(units)]

    def softmax_step(u, s, kb, masked):
        off = kb * tk - q0
        slope = slopes[units[u][0]]
        cols = []
        for c in range(nc):
            sc = s[:, c * LANES:(c + 1) * LANES] + slope * (kcol + (off + c * LANES)).astype(F32)
            if masked:
                sc = jnp.where(row_minus_col >= (off + c * LANES), sc, NEG_BIG)
            cols.append(sc)
        mx = cols[0]
        for sc in cols[1:]:
            mx = jnp.maximum(mx, sc)
        m_prev = m_ref[u]
        m_new = jnp.maximum(m_prev, jnp.max(mx, axis=-1, keepdims=True))
        alpha = jnp.exp2(m_prev - m_new)
        ps = [jnp.exp2(sc - m_new) for sc in cols]
        psum = ps[0]
        for pc in ps[1:]:
            psum = psum + pc
        l_ref[u] = alpha * l_ref[u] + psum
        acc_ref[u] = alpha * acc_ref[u]
        m_ref[u] = m_new
        return jnp.concatenate([pc.astype(BF16) for pc in ps], axis=1)

    odd = lax.rem(qi, 2)
    s0 = scores(0)
    for u in range(len(units)):
        s_ref[odd, u] = s0[u]
        p_ref[odd, u] = jnp.zeros((tq, tk), BF16)

    def pending_pv(j, slot):
        vs = [load(v_ref, jnp.maximum(j - 1, 0), hh) for hh in range(hps)]
        return [_dot(p_ref[slot, u], vs[hh]) for u, (hh, _) in enumerate(units)]

    def step(j, slot):
        nxt = 1 - slot
        pv = pending_pv(j, slot)
        sn = scores(j + 1)
        for u in range(len(units)):
            acc_ref[u] += pv[u]
            s_ref[nxt, u] = sn[u]
        for u in range(len(units)):
            p_ref[nxt, u] = softmax_step(u, s_ref[slot, u], j, False)

    @pl.when(odd == 1)
    def _():
        step(0, 1)

    def body(t, c):
        j = odd + 2 * t
        step(j, 0)
        step(j + 1, 1)
        return c

    lax.fori_loop(0, qi // 2, body, 0)
    pv = pending_pv(qi, 0)
    for u in range(len(units)):
        acc_ref[u] += pv[u]
    vdiag = [load(v_ref, qi, hh) for hh in range(hps)]
    for u, (hh, _) in enumerate(units):
        p = softmax_step(u, s_ref[0, u], qi, True)
        acc_ref[u] += _dot(p, vdiag[hh])

    lp = lam_ref[...]
    s1 = jnp.sum(lp[0:1] * lp[1:2], axis=-1, keepdims=True)
    s2 = jnp.sum(lp[2:3] * lp[3:4], axis=-1, keepdims=True)
    lam = jnp.exp(s1) - jnp.exp(s2) + lambda_init
    for hh in range(hps):
        l0 = jnp.sum(l_ref[2 * hh], axis=-1, keepdims=True)
        l1 = jnp.sum(l_ref[2 * hh + 1], axis=-1, keepdims=True)
        o = acc_ref[2 * hh] / l0 - lam * (acc_ref[2 * hh + 1] / l1)
        o_ref[:, head_cols[hh]] = (_rms(o, sg_ref[...]) * (1.0 - lambda_init)).astype(BF16)


def diff_attn(proj, lam_params, subln_g, bsz, seq, lambda_init, tq=512, hps=1):
    tq = min(tq, seq)
    nq = seq // tq
    n_units = 2 * hps
    w = hps * LANES
    kern = functools.partial(_diff_attn_kernel, tq=tq, hps=hps, lambda_init=lambda_init)
    return pl.pallas_call(
        kern,
        grid=(bsz, B_HEADS // hps, nq),
        in_specs=[
            pl.BlockSpec((4, B_QK_DIM), lambda b, h, i: (0, 0)),
            pl.BlockSpec((1, B_V_DIM), lambda b, h, i: (0, 0)),
            pl.BlockSpec((tq, w), lambda b, h, i: (b * nq + i, COL_QB // hps + h)),
            pl.BlockSpec((seq, w), lambda b, h, i: (b, COL_KB // hps + h)),
            pl.BlockSpec((seq, w), lambda b, h, i: (b, COL_VB // hps + h)),
        ],
        out_specs=pl.BlockSpec((tq, w), lambda b, h, i: (b * nq + i, h)),
        out_shape=jax.ShapeDtypeStruct((bsz * seq, B_HEADS * B_V_DIM), BF16),
        scratch_shapes=[pltpu.VMEM((n_units, tq, LANES), F32), pltpu.VMEM((n_units, tq, LANES), F32),
                        pltpu.VMEM((n_units, tq, B_V_DIM), F32),
                        pltpu.VMEM((2, n_units, tq, tq), F32), pltpu.VMEM((2, n_units, tq, tq), BF16)],
        compiler_params=pltpu.CompilerParams(
            dimension_semantics=("arbitrary", "arbitrary", "arbitrary"), vmem_limit_bytes=VMEM_LIMIT),
        name="diff_attn",
    )(lam_params, subln_g.reshape(1, B_V_DIM), proj, proj, proj)


def _stick_kernel(q_ref, k_ref, v_ref, o_ref, qs_ref, carry_ref, acc_ref, *, tq, n_chains):
    tk = tq
    n_pairs = C_HEADS // 2
    gi = pl.program_id(1)
    lane = lax.broadcasted_iota(jnp.int32, (tq, LANES), 1)
    low = lane < C_HEAD_DIM
    units = [(c, p) for c in range(n_chains) for p in range(n_pairs)]
    for c, p in units:
        q = q_ref[c * tq:(c + 1) * tq, p * LANES:(p + 1) * LANES]
        zero = jnp.zeros_like(q)
        qs_ref[c, p, 0:tq] = jnp.where(low, q, zero)
        qs_ref[c, p, tq:2 * tq] = jnp.where(low, zero, q)
    carry_ref[...] = jnp.zeros(carry_ref.shape, F32)
    acc_ref[...] = jnp.zeros(acc_ref.shape, F32)

    row = lax.broadcasted_iota(jnp.int32, (2 * tq, tk), 0)
    col = lax.broadcasted_iota(jnp.int32, (2 * tq, tk), 1)
    strict = (row % tq) > col
    uj = lax.broadcasted_iota(jnp.int32, (2 * tk, tk + LANES), 0) % tk
    us = lax.broadcasted_iota(jnp.int32, (2 * tk, tk + LANES), 1)
    w2 = jnp.where(jnp.logical_or(uj > us, us >= tk), 1.0, 0.0).astype(BF16)
    vlow = lax.broadcasted_iota(jnp.int32, (tk, LANES), 1) < C_HEAD_DIM

    def step(n, masked):
        kbs = [gi * n_chains + c - n for c in range(n_chains)]
        starts = [pl.multiple_of(jnp.maximum(kb, 0) * tk, tk) for kb in kbs]
        gone = [jnp.where(kb < 0, NEG_BIG, 0.0).astype(F32) for kb in kbs]
        cols = [slice(p * LANES, (p + 1) * LANES) for p in range(n_pairs)]
        zs = [_nt_dot(qs_ref[c, p], k_ref[pl.ds(starts[c], tk), cols[p]]) for c, p in units]
        lbs, rs = [], []
        for z in zs:
            sp = jnp.log(1.0 + jnp.exp(-jnp.abs(z)))
            lb = jnp.minimum(z, 0.0) - sp
            lk = lb - z
            if masked:
                lk = jnp.where(strict, lk, 0.0)
            hi = lk.astype(BF16)
            lo = (lk - hi.astype(F32)).astype(BF16)
            lbs.append(lb)
            rs.append(_dot(jnp.concatenate([hi, lo], axis=1), w2))
        alive = None
        for u, (c, p) in enumerate(units):
            carry = carry_ref[c, p]
            a = jnp.exp(lbs[u] + rs[u][:, :tk] + (carry + gone[c]))
            if masked:
                a = jnp.where(strict, a, 0.0)
            carry = carry + rs[u][:, tk:]
            carry_ref[c, p] = carry
            left = carry + jnp.where(kbs[c] < 1, NEG_BIG, 0.0).astype(F32)
            alive = left if alive is None else jnp.maximum(alive, left)
            ab = a.astype(BF16)
            vv = v_ref[pl.ds(starts[c], tk), cols[p]]
            zero = jnp.zeros_like(vv)
            vst = jnp.concatenate([jnp.where(vlow, vv, zero), jnp.where(vlow, zero, vv)], axis=0)
            acc_ref[c, p] += _dot(jnp.concatenate([ab[:tq], ab[tq:]], axis=1), vst)
        return jnp.max(alive)

    def cond(st):
        _, alive = st
        return alive > EXP_UNDERFLOW

    def body(st):
        n, _ = st
        return n + 1, step(n, False)

    lax.while_loop(cond, body, (1, step(0, True)))
    for c, p in units:
        o_ref[c * tq:(c + 1) * tq, p * LANES:(p + 1) * LANES] = acc_ref[c, p].astype(BF16)


def stick_attn(proj, bsz, seq, tq=128, n_chains=2):
    n_chains = min(n_chains, seq // tq)
    rows = tq * n_chains
    nq = seq // rows
    kern = functools.partial(_stick_kernel, tq=tq, n_chains=n_chains)
    n_pairs = C_HEADS // 2
    return pl.pallas_call(
        kern,
        grid=(bsz, nq),
        in_specs=[
            pl.BlockSpec((rows, C_WIDTH), lambda b, i: (b * nq + i, 5)),
            pl.BlockSpec((seq, C_WIDTH), lambda b, i: (b, 6)),
            pl.BlockSpec((seq, C_WIDTH), lambda b, i: (b, 7)),
        ],
        out_specs=pl.BlockSpec((rows, C_WIDTH), lambda b, i: (b * nq + i, 0)),
        out_shape=jax.ShapeDtypeStruct((bsz * seq, C_WIDTH), BF16),
        scratch_shapes=[pltpu.VMEM((n_chains, n_pairs, 2 * tq, LANES), BF16),
                        pltpu.VMEM((n_chains, n_pairs, 2 * tq, LANES), F32),
                        pltpu.VMEM((n_chains, n_pairs, tq, LANES), F32)],
        compiler_params=pltpu.CompilerParams(
            dimension_semantics=("arbitrary", "arbitrary"), vmem_limit_bytes=VMEM_LIMIT),
        name="stick_attn",
    )(proj, proj, proj)


def _top2_route(sel, aff):
    def top2_sum(a, b, c, d):
        hi1, lo1 = jnp.maximum(a, b), jnp.minimum(a, b)
        hi2, lo2 = jnp.maximum(c, d), jnp.minimum(c, d)
        return jnp.maximum(hi1, hi2) + jnp.maximum(jnp.minimum(hi1, hi2), jnp.maximum(lo1, lo2))

    scores = [top2_sum(*sel[4 * g:4 * g + 4]) for g in range(N_GROUPS)]
    best = jnp.zeros_like(scores[0], dtype=jnp.int32)
    best_score = scores[0]
    for g in range(1, N_GROUPS):
        better = scores[g] > best_score
        best = jnp.where(better, g, best)
        best_score = jnp.where(better, scores[g], best_score)

    def pick(rows, i):
        out = rows[i]
        for g in range(1, N_GROUPS):
            out = jnp.where(best == g, rows[4 * g + i], out)
        return out

    cs = [pick(sel, i) for i in range(EXPERTS_PER_GROUP)]
    ca = [pick(aff, i) for i in range(EXPERTS_PER_GROUP)]

    def argmax_first(vals, exclude=None):
        idx = jnp.full(vals[0].shape, -1, jnp.int32)
        cur = jnp.full(vals[0].shape, -jnp.inf, F32)
        for i, v in enumerate(vals):
            ok = v > cur
            if exclude is not None:
                ok = jnp.logical_and(ok, exclude != i)
            idx = jnp.where(ok, i, idx)
            cur = jnp.where(ok, v, cur)
        return idx

    i1 = argmax_first(cs)
    i2 = argmax_first(cs, exclude=i1)

    def take(vals, idx):
        out = vals[0]
        for i in range(1, len(vals)):
            out = jnp.where(idx == i, vals[i], out)
        return out

    w1, w2 = take(ca, i1), take(ca, i2)
    tot = w1 + w2
    w1, w2 = w1 / tot, w2 / tot
    e1 = best * EXPERTS_PER_GROUP + i1
    e2 = best * EXPERTS_PER_GROUP + i2
    gates = [jnp.where(e1 == e, w1, 0.0) + jnp.where(e2 == e, w2, 0.0) for e in range(N_EXPERTS)]
    return gates, best


def _merge_kernel(x_ref, u_ref, v_ref, g0_ref, g1_ref, g2_ref, yb_ref, yc_ref,
                  ws_ref, bst_ref, gv_ref, wa_ref, wb_ref, wc_ref, wo_ref, ng_ref, rwt_ref, rb_ref,
                  h_ref, xe_ref, grp_ref, rank_ref, cnt_ref, ya_ref, run_ref, *, tm):
    u = jax.nn.gelu(u_ref[...].astype(F32))
    v = _rms(jax.nn.gelu(v_ref[...].astype(F32)), gv_ref[...]).astype(BF16)

    r = lax.broadcasted_iota(jnp.int32, (BLOCK, BLOCK), 0)
    c = lax.broadcasted_iota(jnp.int32, (BLOCK, BLOCK), 1)
    causal = r >= c
    for g in range(A_GROUPS):
        w = jnp.where(causal, ws_ref[g], 0.0).astype(BF16)
        bias = bst_ref[:, g:g + 1]
        cols = slice(g * LANES, (g + 1) * LANES)
        for ch in range(tm // BLOCK):
            rows = slice(ch * BLOCK, (ch + 1) * BLOCK)
            mixed = _dot(w, v[rows, cols]) + bias
            ya_ref[rows, cols] = (u[rows, cols] * mixed).astype(BF16)

    merged = _sigmoid(g0_ref[...].astype(F32)) * _dot(ya_ref[...], wa_ref[...])
    merged += _sigmoid(g1_ref[...].astype(F32)) * _dot(yb_ref[...], wb_ref[...])
    merged += _sigmoid(g2_ref[...].astype(F32)) * _dot(yc_ref[...], wc_ref[...])
    h = x_ref[...] + _dot(merged.astype(BF16), wo_ref[...])
    h_ref[...] = h

    xn = _rms(h, ng_ref[...])
    xe_ref[:, :D_MODEL] = xn

    xh = xn.astype(BF16)
    xl = (xn - xh.astype(F32)).astype(BF16)
    rw = rwt_ref[...]
    rh = rw.astype(BF16)
    rl = (rw - rh.astype(F32)).astype(BF16)
    logits = _nt_dot(rh, xh) + (_nt_dot(rh, xl) + _nt_dot(rl, xh))
    aff = jax.nn.sigmoid(logits)
    sel = aff + rb_ref[...]
    gates, best = _top2_route([sel[e:e + 1] for e in range(N_EXPERTS)],
                              [aff[e:e + 1] for e in range(N_EXPERTS)])
    gate_rows = jnp.concatenate(gates + [jnp.zeros((LANES - N_EXPERTS, tm), F32)], axis=0)
    xe_ref[:, D_MODEL:] = gate_rows.T

    @pl.when(pl.program_id(0) == 0)
    def _():
        run_ref[...] = jnp.zeros(run_ref.shape, F32)

    gid = lax.broadcasted_iota(jnp.int32, (8, tm), 0)
    member = (gid == best).astype(F32)
    before = lax.broadcasted_iota(jnp.int32, (tm, tm), 0) < lax.broadcasted_iota(jnp.int32, (tm, tm), 1)
    prefix = _dot(member.astype(BF16), before.astype(BF16))
    run = run_ref[...]
    rank = jnp.sum(member * (prefix + run[:, 0:1]), axis=0, keepdims=True)
    rank_ref[...] = rank.astype(jnp.int32)
    grp_ref[...] = best
    run = run + jnp.sum(member, axis=-1, keepdims=True)
    run_ref[...] = run
    cnt_ref[...] = run


def merge(x2, proj, yb, yc, w_s, b_s_t, g_v, wa, wb, wc, wo, norm_g, router_wt, router_b, tm=512):
    t, d = x2.shape
    tm = min(tm, t)
    row = lambda i: (i, 0)
    const2 = lambda i: (0, 0)
    kern = functools.partial(_merge_kernel, tm=tm)
    return pl.pallas_call(
        kern,
        grid=(t // tm,),
        in_specs=[
            pl.BlockSpec((tm, d), row),
            pl.BlockSpec((tm, A_WIDTH), lambda i: (i, 0)),
            pl.BlockSpec((tm, A_WIDTH), lambda i: (i, 1)),
            pl.BlockSpec((tm, d), lambda i: (i, 4)),
            pl.BlockSpec((tm, d), lambda i: (i, 5)),
            pl.BlockSpec((tm, d), lambda i: (i, 6)),
            pl.BlockSpec((tm, A_WIDTH), row),
            pl.BlockSpec((tm, C_WIDTH), row),
            pl.BlockSpec((A_GROUPS, BLOCK, BLOCK), lambda i: (0, 0, 0)),
            pl.BlockSpec((BLOCK, A_GROUPS), const2),
            pl.BlockSpec((1, A_WIDTH), const2),
            pl.BlockSpec((A_WIDTH, d), const2),
            pl.BlockSpec((A_WIDTH, d), const2),
            pl.BlockSpec((C_WIDTH, d), const2),
            pl.BlockSpec((d, d), const2),
            pl.BlockSpec((1, d), const2),
            pl.BlockSpec((N_EXPERTS, d), const2),
            pl.BlockSpec((N_EXPERTS, 1), const2),
        ],
        out_specs=[
            pl.BlockSpec((tm, d), row),
            pl.BlockSpec((tm, XE_WIDTH), row),
            pl.BlockSpec((1, tm), lambda i: (0, i)),
            pl.BlockSpec((1, tm), lambda i: (0, i)),
            pl.BlockSpec((8, LANES), const2),
        ],
        out_shape=[
            jax.ShapeDtypeStruct((t, d), F32),
            jax.ShapeDtypeStruct((t, XE_WIDTH), F32),
            jax.ShapeDtypeStruct((1, t), jnp.int32),
            jax.ShapeDtypeStruct((1, t), jnp.int32),
            jax.ShapeDtypeStruct((8, LANES), F32),
        ],
        scratch_shapes=[pltpu.VMEM((tm, A_WIDTH), BF16), pltpu.VMEM((8, LANES), F32)],
        compiler_params=pltpu.CompilerParams(
            dimension_semantics=("arbitrary",), vmem_limit_bytes=VMEM_LIMIT),
        name="merge",
    )(x2, proj, proj, proj, proj, proj, yb, yc, w_s, b_s_t, g_v.reshape(1, A_WIDTH),
      wa, wb, wc, wo, norm_g.reshape(1, d), router_wt, router_b.reshape(N_EXPERTS, 1))


def _route_tables(cnt, grp, rank, t, rt):
    cnt = cnt[:N_GROUPS, 0].astype(jnp.int32)
    padded = (cnt + rt - 1) // rt * rt
    off_end = jnp.cumsum(padded)
    off = off_end - padded
    g = grp.reshape(t)
    pos = rank.reshape(t)
    for k in range(N_GROUPS):
        pos = pos + jnp.where(g == k, off[k], 0)
    starts = jnp.arange(t // rt + N_GROUPS, dtype=jnp.int32) * rt
    tile_group = jnp.minimum(jnp.sum((starts[:, None] >= off_end[None, :]).astype(jnp.int32), axis=1),
                             N_GROUPS - 1)
    tile_valid = (starts < off_end[N_GROUPS - 1]).astype(jnp.int32)
    return pos, tile_group, tile_valid


ROW_DMA_UNROLL = 8


def _row_copy(src_ref, src_row, dst_ref, dst_row, sem):
    return pltpu.make_async_copy(src_ref.at[pl.ds(src_row, 1)], dst_ref.at[pl.ds(dst_row, 1)], sem)


def _dispatch_kernel(pos_ref, xe_ref, xs_init_ref, xs_ref, sem, *, tm):
    del xs_init_ref
    base = pl.program_id(0) * tm

    def issue(r, c):
        _row_copy(xe_ref, r, xs_ref, pos_ref[base + r], sem).start()
        return c

    lax.fori_loop(0, tm, issue, 0, unroll=ROW_DMA_UNROLL)
    pltpu.make_async_copy(xe_ref, xs_ref.at[pl.ds(0, tm)], sem).wait()


def moe_dispatch(pos, xe, t_pad, tm=512):
    t, w = xe.shape
    tm = min(tm, t)
    kern = functools.partial(_dispatch_kernel, tm=tm)
    return pl.pallas_call(
        kern,
        grid_spec=pltpu.PrefetchScalarGridSpec(
            num_scalar_prefetch=1,
            grid=(t // tm,),
            in_specs=[pl.BlockSpec((tm, w), lambda i, pos: (i, 0)),
                      pl.BlockSpec(memory_space=pl.ANY)],
            out_specs=pl.BlockSpec(memory_space=pl.ANY),
            scratch_shapes=[pltpu.SemaphoreType.DMA],
        ),
        out_shape=jax.ShapeDtypeStruct((t_pad, w), F32),
        input_output_aliases={2: 0},
        compiler_params=pltpu.CompilerParams(
            dimension_semantics=("arbitrary",), vmem_limit_bytes=VMEM_LIMIT),
        name="moe_dispatch",
    )(pos, xe, jnp.zeros((t_pad, w), F32))


def _moe_ffn_kernel(tg_ref, tv_ref, xs_ref, w1_ref, w3_ref, w2_ref, ys_ref):
    i = pl.program_id(0)
    valid = tv_ref[i] == 1

    @pl.when(valid)
    def _():
        x = xs_ref[:, :D_MODEL].astype(BF16)
        ext = xs_ref[:, D_MODEL:]
        lane = lax.broadcasted_iota(jnp.int32, ext.shape, 1)
        first = tg_ref[i] * EXPERTS_PER_GROUP

        def up(e):
            return _dot(x, w1_ref[e]), _dot(x, w3_ref[e])

        def down(e, ab):
            gcol = jnp.sum(jnp.where(lane == first + e, ext, 0.0), axis=-1, keepdims=True)
            hh = ((jax.nn.silu(ab[0]) * ab[1]) * gcol).astype(BF16)
            return _dot(hh, w2_ref[e])

        ab = up(0)
        out = None
        for e in range(EXPERTS_PER_GROUP):
            nxt = up(e + 1) if e + 1 < EXPERTS_PER_GROUP else None
            o = down(e, ab)
            out = o if out is None else out + o
            ab = nxt
        ys_ref[...] = out

    @pl.when(jnp.logical_not(valid))
    def _():
        ys_ref[...] = jnp.zeros(ys_ref.shape, F32)


def moe_ffn(tile_group, tile_valid, xs, w1, w3, w2, rt):
    t_pad, w = xs.shape
    d = D_MODEL
    w1, w3, w2 = (a.reshape(N_GROUPS, EXPERTS_PER_GROUP, *a.shape[1:]) for a in (w1, w3, w2))
    widx = lambda i, tg, tv: (tg[i], 0, 0, 0)
    return pl.pallas_call(
        _moe_ffn_kernel,
        grid_spec=pltpu.PrefetchScalarGridSpec(
            num_scalar_prefetch=2,
            grid=(t_pad // rt,),
            in_specs=[pl.BlockSpec((rt, w), lambda i, tg, tv: (i, 0)),
                      pl.BlockSpec((None, EXPERTS_PER_GROUP, d, D_FF_EXPERT), widx),
                      pl.BlockSpec((None, EXPERTS_PER_GROUP, d, D_FF_EXPERT), widx),
                      pl.BlockSpec((None, EXPERTS_PER_GROUP, D_FF_EXPERT, d), widx)],
            out_specs=pl.BlockSpec((rt, d), lambda i, tg, tv: (i, 0)),
        ),
        out_shape=jax.ShapeDtypeStruct((t_pad, d), F32),
        compiler_params=pltpu.CompilerParams(
            dimension_semantics=("arbitrary",), vmem_limit_bytes=VMEM_LIMIT),
        name="moe_ffn",
    )(tile_group, tile_valid, xs, w1, w3, w2)


def _combine_kernel(pos_ref, h_ref, ys_ref, fg_ref, o_ref, ybuf_ref, sem, *, tm, final_norm):
    base = pl.program_id(0) * tm

    def issue(r, c):
        _row_copy(ys_ref, pos_ref[base + r], ybuf_ref, r, sem).start()
        return c

    lax.fori_loop(0, tm, issue, 0, unroll=ROW_DMA_UNROLL)
    pltpu.make_async_copy(ys_ref.at[pl.ds(0, tm)], ybuf_ref, sem).wait()
    y = h_ref[...] + ybuf_ref[...]
    if final_norm:
        y = _rms(y, fg_ref[...])
    o_ref[...] = y


def moe_combine(pos, h, ys, final_gain, final_norm, tm=512):
    t, d = h.shape
    tm = min(tm, t)
    kern = functools.partial(_combine_kernel, tm=tm, final_norm=final_norm)
    return pl.pallas_call(
        kern,
        grid_spec=pltpu.PrefetchScalarGridSpec(
            num_scalar_prefetch=1,
            grid=(t // tm,),
            in_specs=[pl.BlockSpec((tm, d), lambda i, pos: (i, 0)),
                      pl.BlockSpec(memory_space=pl.ANY),
                      pl.BlockSpec((1, d), lambda i, pos: (0, 0))],
            out_specs=pl.BlockSpec((tm, d), lambda i, pos: (i, 0)),
            scratch_shapes=[pltpu.VMEM((tm, d), F32), pltpu.SemaphoreType.DMA],
        ),
        out_shape=jax.ShapeDtypeStruct((t, d), F32),
        compiler_params=pltpu.CompilerParams(
            dimension_semantics=("arbitrary",), vmem_limit_bytes=VMEM_LIMIT),
        name="moe_combine",
    )(pos, h, ys, final_gain.reshape(1, d))


def grouped_moe(h, xe, grp, rank, cnt, w1, w3, w2, final_gain, final_norm, rt=512):
    t = h.shape[0]
    rt = min(rt, t)
    pos, tile_group, tile_valid = _route_tables(cnt, grp, rank, t, rt)
    xs = moe_dispatch(pos, xe, t + N_GROUPS * rt)
    ys = moe_ffn(tile_group, tile_valid, xs, w1, w3, w2, rt)
    return moe_combine(pos, h, ys, final_gain, final_norm)


def kernel(x, norm_mix_gain, w_in, gmlp_w_s, gmlp_b_s, gmlp_v_gain, diff_lambda, diff_subln_gain,
           w_up_a, w_up_b, w_up_c, w_out, norm_ffn_gain, router_w, router_bias,
           moe_w1, moe_w3, moe_w2, final_gain):
    bsz, seq, d = x.shape
    depth = w_in.shape[0]
    x2 = x.reshape(bsz * seq, d)
    router_wt = router_w.T
    for l in range(depth):
        lambda_init = 0.8 - 0.6 * math.exp(-0.3 * l)
        proj = in_proj(x2, norm_mix_gain[l], w_in[l].astype(BF16))
        yb = diff_attn(proj, diff_lambda[l], diff_subln_gain[l], bsz, seq, lambda_init)
        yc = stick_attn(proj, bsz, seq)
        h, xe, grp, rank, cnt = merge(x2, proj, yb, yc, gmlp_w_s[l], gmlp_b_s[l].T, gmlp_v_gain[l],
                                      w_up_a[l].astype(BF16), w_up_b[l].astype(BF16), w_up_c[l].astype(BF16),
                                      w_out[l].astype(BF16), norm_ffn_gain[l], router_wt, router_bias)
        x2 = grouped_moe(h, xe, grp, rank, cnt, moe_w1[l].astype(BF16), moe_w3[l].astype(BF16),
                         moe_w2[l].astype(BF16), final_gain, final_norm=(l == depth - 1))
    return x2.reshape(bsz, seq, d)
```

```python
import functools
import math

import jax
import jax.numpy as jnp
from jax import lax
from jax.experimental import pallas as pl
from jax.experimental.pallas import tpu as pltpu

D_MODEL = 1024
BLOCK = 128
A_GROUPS = 4
A_WIDTH = 512
B_HEADS = 4
B_QK_DIM = 64
B_V_DIM = 128
C_HEADS = 8
C_HEAD_DIM = 64
C_WIDTH = 512
N_EXPERTS = 16
N_GROUPS = 4
EXPERTS_PER_GROUP = 4
D_FF_EXPERT = 512
IN_WIDTH = 7168
XE_WIDTH = D_MODEL + 128
EPS = 1e-6

LANES = 128
VMEM_LIMIT = 56 * 1024 * 1024
F32 = jnp.float32
BF16 = jnp.bfloat16
NEG_BIG = -1e30
EXP_UNDERFLOW = -104.0

LOG2E = 1.4426950408889634

COL_QB, COL_KB, COL_VB = 8, 12, 16
QB_COLS = (2 * A_WIDTH, 2 * A_WIDTH + 512)
QC_COLS = (2 * A_WIDTH + 3 * 512, 2 * A_WIDTH + 4 * 512)


def _nt_dot(a, b):
    return lax.dot_general(a, b, (((1,), (1,)), ((), ())), preferred_element_type=F32)


def _dot(a, b):
    return jnp.dot(a, b, preferred_element_type=F32)


def _sigmoid(x):
    return 0.5 * jnp.tanh(0.5 * x) + 0.5


def _rms(xf, gain):
    return xf * lax.rsqrt(jnp.mean(xf * xf, axis=-1, keepdims=True) + EPS) * gain


def _in_proj_kernel(x_ref, g_ref, w_ref, cs_ref, o_ref, xn_ref):
    @pl.when(pl.program_id(1) == 0)
    def _():
        xn_ref[...] = _rms(x_ref[...], g_ref[...]).astype(BF16)

    o_ref[...] = (_dot(xn_ref[...], w_ref[...]) * cs_ref[...]).astype(BF16)


def _query_scale_row(n):
    col = jnp.arange(n)
    s = jnp.ones((n,), F32)
    s = jnp.where((col >= QB_COLS[0]) & (col < QB_COLS[1]), LOG2E * B_QK_DIM ** -0.5, s)
    s = jnp.where((col >= QC_COLS[0]) & (col < QC_COLS[1]), C_HEAD_DIM ** -0.5, s)
    return s.reshape(1, n)


def in_proj(x2, gain, w_bf16, tm=1024, tn=1792):
    t, d = x2.shape
    n = w_bf16.shape[1]
    tm = min(tm, t)
    return pl.pallas_call(
        _in_proj_kernel,
        grid=(t // tm, n // tn),
        in_specs=[
            pl.BlockSpec((tm, d), lambda i, j: (i, 0)),
            pl.BlockSpec((1, d), lambda i, j: (0, 0)),
            pl.BlockSpec((d, tn), lambda i, j: (0, j)),
            pl.BlockSpec((1, tn), lambda i, j: (0, j)),
        ],
        out_specs=pl.BlockSpec((tm, tn), lambda i, j: (i, j)),
        out_shape=jax.ShapeDtypeStruct((t, n), BF16),
        scratch_shapes=[pltpu.VMEM((tm, d), BF16)],
        compiler_params=pltpu.CompilerParams(
            dimension_semantics=("arbitrary", "arbitrary"), vmem_limit_bytes=VMEM_LIMIT),
        name="in_proj",
    )(x2, gain.reshape(1, d), w_bf16, _query_scale_row(n))


def _diff_attn_kernel(lam_ref, sg_ref, q_ref, k_ref, v_ref, o_ref, m_ref, l_ref, acc_ref, s_ref, p_ref,
                      *, tq, hps, lambda_init):
    tk = tq
    hp = pl.program_id(1)
    qi = pl.program_id(2)
    nc = tk // LANES
    units = [(hh, mi) for hh in range(hps) for mi in range(2)]
    head_cols = [slice(hh * LANES, (hh + 1) * LANES) for hh in range(hps)]

    def head_slope(h):
        sl = jnp.where(h == 0, 0.25, jnp.where(h == 1, 0.0625, jnp.where(h == 2, 0.015625, 0.00390625)))
        return sl.astype(F32) * LOG2E

    slopes = [head_slope(hp * hps + hh) for hh in range(hps)]

    lane = lax.broadcasted_iota(jnp.int32, (tq, LANES), 1)
    qs = []
    for hh, mi in units:
        q = q_ref[:, head_cols[hh]]
        keep = (lane < B_QK_DIM) if mi == 0 else (lane >= B_QK_DIM)
        qs.append(jnp.where(keep, q, jnp.zeros_like(q)))

    m_ref[...] = jnp.full(m_ref.shape, NEG_BIG, F32)
    l_ref[...] = jnp.zeros(l_ref.shape, F32)
    acc_ref[...] = jnp.zeros(acc_ref.shape, F32)

    kcol = lax.broadcasted_iota(jnp.int32, (1, LANES), 1)
    row_minus_col = lax.broadcasted_iota(jnp.int32, (tq, LANES), 0) - lane
    q0 = qi * tq

    def load(ref, kb, hh):
        return ref[pl.ds(pl.multiple_of(kb * tk, tk), tk), head_cols[hh]]

    def scores(kb):
        kks = [load(k_ref, kb, hh) for hh in range(hps)]
        return [_nt_dot(qs[u], kks[hh]) for u, (hh, _) in enumerate(units)]

    def softmax_step(u, s, kb, masked):
        off = kb * tk - q0
        slope = slopes[units[u][0]]
        cols = []
        for c in range(nc):
            sc = s[:, c * LANES:(c + 1) * LANES] + slope * (kcol + (off + c * LANES)).astype(F32)
            if masked:
                sc = jnp.where(row_minus_col >= (off + c * LANES), sc, NEG_BIG)
            cols.append(sc)
        mx = cols[0]
        for sc in cols[1:]:
            mx = jnp.maximum(mx, sc)
        m_prev = m_ref[u]
        m_new = jnp.maximum(m_prev, jnp.max(mx, axis=-1, keepdims=True))
        alpha = jnp.exp2(m_prev - m_new)
        ps = [jnp.exp2(sc - m_new) for sc in cols]
        psum = ps[0]
        for pc in ps[1:]:
            psum = psum + pc
        l_ref[u] = alpha * l_ref[u] + psum
        acc_ref[u] = alpha * acc_ref[u]
        m_ref[u] = m_new
        return jnp.concatenate([pc.astype(BF16) for pc in ps], axis=1)

    odd = lax.rem(qi, 2)
    s0 = scores(0)
    for u in range(len(units)):
        s_ref[odd, u] = s0[u]
        p_ref[odd, u] = jnp.zeros((tq, tk), BF16)

    def pending_pv(j, slot):
        vs = [load(v_ref, jnp.maximum(j - 1, 0), hh) for hh in range(hps)]
        return [_dot(p_ref[slot, u], vs[hh]) for u, (hh, _) in enumerate(units)]

    def step(j, slot):
        nxt = 1 - slot
        pv = pending_pv(j, slot)
        sn = scores(j + 1)
        for u in range(len(units)):
            acc_ref[u] += pv[u]
            s_ref[nxt, u] = sn[u]
        for u in range(len(units)):
            p_ref[nxt, u] = softmax_step(u, s_ref[slot, u], j, False)

    @pl.when(odd == 1)
    def _():
        step(0, 1)

    def body(t, c):
        j = odd + 2 * t
        step(j, 0)
        step(j + 1, 1)
        return c

    lax.fori_loop(0, qi // 2, body, 0)
    pv = pending_pv(qi, 0)
    for u in range(len(units)):
        acc_ref[u] += pv[u]
    vdiag = [load(v_ref, qi, hh) for hh in range(hps)]
    for u, (hh, _) in enumerate(units):
        p = softmax_step(u, s_ref[0, u], qi, True)
        acc_ref[u] += _dot(p, vdiag[hh])

    lp = lam_ref[...]
    s1 = jnp.sum(lp[0:1] * lp[1:2], axis=-1, keepdims=True)
    s2 = jnp.sum(lp[2:3] * lp[3:4], axis=-1, keepdims=True)
    lam = jnp.exp(s1) - jnp.exp(s2) + lambda_init
    for hh in range(hps):
        l0 = jnp.sum(l_ref[2 * hh], axis=-1, keepdims=True)
        l1 = jnp.sum(l_ref[2 * hh + 1], axis=-1, keepdims=True)
        o = acc_ref[2 * hh] / l0 - lam * (acc_ref[2 * hh + 1] / l1)
        o_ref[:, head_cols[hh]] = (_rms(o, sg_ref[...]) * (1.0 - lambda_init)).astype(BF16)


def diff_attn(proj, lam_params, subln_g, bsz, seq, lambda_init, tq=512, hps=1):
    tq = min(tq, seq)
    nq = seq // tq
    n_units = 2 * hps
    w = hps * LANES
    kern = functools.partial(_diff_attn_kernel, tq=tq, hps=hps, lambda_init=lambda_init)
    return pl.pallas_call(
        kern,
        grid=(bsz, B_HEADS // hps, nq),
        in_specs=[
            pl.BlockSpec((4, B_QK_DIM), lambda b, h, i: (0, 0)),
            pl.BlockSpec((1, B_V_DIM), lambda b, h, i: (0, 0)),
            pl.BlockSpec((tq, w), lambda b, h, i: (b * nq + i, COL_QB // hps + h)),
            pl.BlockSpec((seq, w), lambda b, h, i: (b, COL_KB // hps + h)),
            pl.BlockSpec((seq, w), lambda b, h, i: (b, COL_VB // hps + h)),
        ],
        out_specs=pl.BlockSpec((tq, w), lambda b, h, i: (b * nq + i, h)),
        out_shape=jax.ShapeDtypeStruct((bsz * seq, B_HEADS * B_V_DIM), BF16),
        scratch_shapes=[pltpu.VMEM((n_units, tq, LANES), F32), pltpu.VMEM((n_units, tq, LANES), F32),
                        pltpu.VMEM((n_units, tq, B_V_DIM), F32),
                        pltpu.VMEM((2, n_units, tq, tq), F32), pltpu.VMEM((2, n_units, tq, tq), BF16)],
        compiler_params=pltpu.CompilerParams(
            dimension_semantics=("arbitrary", "arbitrary", "arbitrary"), vmem_limit_bytes=VMEM_LIMIT),
        name="diff_attn",
    )(lam_params, subln_g.reshape(1, B_V_DIM), proj, proj, proj)


def _stick_kernel(q_ref, k_ref, v_ref, o_ref, qs_ref, carry_ref, acc_ref, *, tq, n_chains):
    tk = tq
    n_pairs = C_HEADS // 2
    gi = pl.program_id(1)
    lane = lax.broadcasted_iota(jnp.int32, (tq, LANES), 1)
    low = lane < C_HEAD_DIM
    units = [(c, p) for c in range(n_chains) for p in range(n_pairs)]
    for c, p in units:
        q = q_ref[c * tq:(c + 1) * tq, p * LANES:(p + 1) * LANES]
        zero = jnp.zeros_like(q)
        qs_ref[c, p, 0:tq] = jnp.where(low, q, zero)
        qs_ref[c, p, tq:2 * tq] = jnp.where(low, zero, q)
    carry_ref[...] = jnp.zeros(carry_ref.shape, F32)
    acc_ref[...] = jnp.zeros(acc_ref.shape, F32)

    row = lax.broadcasted_iota(jnp.int32, (2 * tq, tk), 0)
    col = lax.broadcasted_iota(jnp.int32, (2 * tq, tk), 1)
    strict = (row % tq) > col
    uj = lax.broadcasted_iota(jnp.int32, (2 * tk, tk + LANES), 0) % tk
    us = lax.broadcasted_iota(jnp.int32, (2 * tk, tk + LANES), 1)
    w2 = jnp.where(jnp.logical_or(uj > us, us >= tk), 1.0, 0.0).astype(BF16)
    vlow = lax.broadcasted_iota(jnp.int32, (tk, LANES), 1) < C_HEAD_DIM

    def step(n, masked):
        kbs = [gi * n_chains + c - n for c in range(n_chains)]
        starts = [pl.multiple_of(jnp.maximum(kb, 0) * tk, tk) for kb in kbs]
        gone = [jnp.where(kb < 0, NEG_BIG, 0.0).astype(F32) for kb in kbs]
        cols = [slice(p * LANES, (p + 1) * LANES) for p in range(n_pairs)]
        zs = [_nt_dot(qs_ref[c, p], k_ref[pl.ds(starts[c], tk), cols[p]]) for c, p in units]
        lbs, rs = [], []
        for z in zs:
            sp = jnp.log(1.0 + jnp.exp(-jnp.abs(z)))
            lb = jnp.minimum(z, 0.0) - sp
            lk = lb - z
            if masked:
                lk = jnp.where(strict, lk, 0.0)
            hi = lk.astype(BF16)
            lo = (lk - hi.astype(F32)).astype(BF16)
            lbs.append(lb)
            rs.append(_dot(jnp.concatenate([hi, lo], axis=1), w2))
        alive = None
        for u, (c, p) in enumerate(units):
            carry = carry_ref[c, p]
            a = jnp.exp(lbs[u] + rs[u][:, :tk] + (carry + gone[c]))
            if masked:
                a = jnp.where(strict, a, 0.0)
            carry = carry + rs[u][:, tk:]
            carry_ref[c, p] = carry
            left = carry + jnp.where(kbs[c] < 1, NEG_BIG, 0.0).astype(F32)
            alive = left if alive is None else jnp.maximum(alive, left)
            ab = a.astype(BF16)
            vv = v_ref[pl.ds(starts[c], tk), cols[p]]
            zero = jnp.zeros_like(vv)
            vst = jnp.concatenate([jnp.where(vlow, vv, zero), jnp.where(vlow, zero, vv)], axis=0)
            acc_ref[c, p] += _dot(jnp.concatenate([ab[:tq], ab[tq:]], axis=1), vst)
        return jnp.max(alive)

    def cond(st):
        _, alive = st
        return alive > EXP_UNDERFLOW

    def body(st):
        n, _ = st
        return n + 1, step(n, False)

    lax.while_loop(cond, body, (1, step(0, True)))
    for c, p in units:
        o_ref[c * tq:(c + 1) * tq, p * LANES:(p + 1) * LANES] = acc_ref[c, p].astype(BF16)


def stick_attn(proj, bsz, seq, tq=128, n_chains=2):
    n_chains = min(n_chains, seq // tq)
    rows = tq * n_chains
    nq = seq // rows
    kern = functools.partial(_stick_kernel, tq=tq, n_chains=n_chains)
    n_pairs = C_HEADS // 2
    return pl.pallas_call(
        kern,
        grid=(bsz, nq),
        in_specs=[
            pl.BlockSpec((rows, C_WIDTH), lambda b, i: (b * nq + i, 5)),
            pl.BlockSpec((seq, C_WIDTH), lambda b, i: (b, 6)),
            pl.BlockSpec((seq, C_WIDTH), lambda b, i: (b, 7)),
        ],
        out_specs=pl.BlockSpec((rows, C_WIDTH), lambda b, i: (b * nq + i, 0)),
        out_shape=jax.ShapeDtypeStruct((bsz * seq, C_WIDTH), BF16),
        scratch_shapes=[pltpu.VMEM((n_chains, n_pairs, 2 * tq, LANES), BF16),
                        pltpu.VMEM((n_chains, n_pairs, 2 * tq, LANES), F32),
                        pltpu.VMEM((n_chains, n_pairs, tq, LANES), F32)],
        compiler_params=pltpu.CompilerParams(
            dimension_semantics=("arbitrary", "arbitrary"), vmem_limit_bytes=VMEM_LIMIT),
        name="stick_attn",
    )(proj, proj, proj)


def _top2_route(sel, aff):
    def top2_sum(a, b, c, d):
        hi1, lo1 = jnp.maximum(a, b), jnp.minimum(a, b)
        hi2, lo2 = jnp.maximum(c, d), jnp.minimum(c, d)
        return jnp.maximum(hi1, hi2) + jnp.maximum(jnp.minimum(hi1, hi2), jnp.maximum(lo1, lo2))

    scores = [top2_sum(*sel[4 * g:4 * g + 4]) for g in range(N_GROUPS)]
    best = jnp.zeros_like(scores[0], dtype=jnp.int32)
    best_score = scores[0]
    for g in range(1, N_GROUPS):
        better = scores[g] > best_score
        best = jnp.where(better, g, best)
        best_score = jnp.where(better, scores[g], best_score)

    def pick(rows, i):
        out = rows[i]
        for g in range(1, N_GROUPS):
            out = jnp.where(best == g, rows[4 * g + i], out)
        return out

    cs = [pick(sel, i) for i in range(EXPERTS_PER_GROUP)]
    ca = [pick(aff, i) for i in range(EXPERTS_PER_GROUP)]

    def argmax_first(vals, exclude=None):
        idx = jnp.full(vals[0].shape, -1, jnp.int32)
        cur = jnp.full(vals[0].shape, -jnp.inf, F32)
        for i, v in enumerate(vals):
            ok = v > cur
            if exclude is not None:
                ok = jnp.logical_and(ok, exclude != i)
            idx = jnp.where(ok, i, idx)
            cur = jnp.where(ok, v, cur)
        return idx

    i1 = argmax_first(cs)
    i2 = argmax_first(cs, exclude=i1)

    def take(vals, idx):
        out = vals[0]
        for i in range(1, len(vals)):
            out = jnp.where(idx == i, vals[i], out)
        return out

    w1, w2 = take(ca, i1), take(ca, i2)
    tot = w1 + w2
    w1, w2 = w1 / tot, w2 / tot
    e1 = best * EXPERTS_PER_GROUP + i1
    e2 = best * EXPERTS_PER_GROUP + i2
    gates = [jnp.where(e1 == e, w1, 0.0) + jnp.where(e2 == e, w2, 0.0) for e in range(N_EXPERTS)]
    return gates, best


def _merge_kernel(x_ref, u_ref, v_ref, g0_ref, g1_ref, g2_ref, yb_ref, yc_ref,
                  ws_ref, bst_ref, gv_ref, wa_ref, wb_ref, wc_ref, wo_ref, ng_ref, rwt_ref, rb_ref,
                  h_ref, xe_ref, grp_ref, rank_ref, cnt_ref, ya_ref, run_ref, *, tm):
    u = jax.nn.gelu(u_ref[...].astype(F32))
    v = _rms(jax.nn.gelu(v_ref[...].astype(F32)), gv_ref[...]).astype(BF16)

    r = lax.broadcasted_iota(jnp.int32, (BLOCK, BLOCK), 0)
    c = lax.broadcasted_iota(jnp.int32, (BLOCK, BLOCK), 1)
    causal = r >= c
    for g in range(A_GROUPS):
        w = jnp.where(causal, ws_ref[g], 0.0).astype(BF16)
        bias = bst_ref[:, g:g + 1]
        cols = slice(g * LANES, (g + 1) * LANES)
        for ch in range(tm // BLOCK):
            rows = slice(ch * BLOCK, (ch + 1) * BLOCK)
            mixed = _dot(w, v[rows, cols]) + bias
            ya_ref[rows, cols] = (u[rows, cols] * mixed).astype(BF16)

    merged = _sigmoid(g0_ref[...].astype(F32)) * _dot(ya_ref[...], wa_ref[...])
    merged += _sigmoid(g1_ref[...].astype(F32)) * _dot(yb_ref[...], wb_ref[...])
    merged += _sigmoid(g2_ref[...].astype(F32)) * _dot(yc_ref[...], wc_ref[...])
    h = x_ref[...] + _dot(merged.astype(BF16), wo_ref[...])
    h_ref[...] = h

    xn = _rms(h, ng_ref[...])
    xe_ref[:, :D_MODEL] = xn

    xh = xn.astype(BF16)
    xl = (xn - xh.astype(F32)).astype(BF16)
    rw = rwt_ref[...]
    rh = rw.astype(BF16)
    rl = (rw - rh.astype(F32)).astype(BF16)
    logits = _nt_dot(rh, xh) + (_nt_dot(rh, xl) + _nt_dot(rl, xh))
    aff = jax.nn.sigmoid(logits)
    sel = aff + rb_ref[...]
    gates, best = _top2_route([sel[e:e + 1] for e in range(N_EXPERTS)],
                              [aff[e:e + 1] for e in range(N_EXPERTS)])
    gate_rows = jnp.concatenate(gates + [jnp.zeros((LANES - N_EXPERTS, tm), F32)], axis=0)
    xe_ref[:, D_MODEL:] = gate_rows.T

    @pl.when(pl.program_id(0) == 0)
    def _():
        run_ref[...] = jnp.zeros(run_ref.shape, F32)

    gid = lax.broadcasted_iota(jnp.int32, (8, tm), 0)
    member = (gid == best).astype(F32)
    before = lax.broadcasted_iota(jnp.int32, (tm, tm), 0) < lax.broadcasted_iota(jnp.int32, (tm, tm), 1)
    prefix = _dot(member.astype(BF16), before.astype(BF16))
    run = run_ref[...]
    rank = jnp.sum(member * (prefix + run[:, 0:1]), axis=0, keepdims=True)
    rank_ref[...] = rank.astype(jnp.int32)
    grp_ref[...] = best
    run = run + jnp.sum(member, axis=-1, keepdims=True)
    run_ref[...] = run
    cnt_ref[...] = run


def merge(x2, proj, yb, yc, w_s, b_s_t, g_v, wa, wb, wc, wo, norm_g, router_wt, router_b, tm=512):
    t, d = x2.shape
    tm = min(tm, t)
    row = lambda i: (i, 0)
    const2 = lambda i: (0, 0)
    kern = functools.partial(_merge_kernel, tm=tm)
    return pl.pallas_call(
        kern,
        grid=(t // tm,),
        in_specs=[
            pl.BlockSpec((tm, d), row),
            pl.BlockSpec((tm, A_WIDTH), lambda i: (i, 0)),
            pl.BlockSpec((tm, A_WIDTH), lambda i: (i, 1)),
            pl.BlockSpec((tm, d), lambda i: (i, 4)),
            pl.BlockSpec((tm, d), lambda i: (i, 5)),
            pl.BlockSpec((tm, d), lambda i: (i, 6)),
            pl.BlockSpec((tm, A_WIDTH), row),
            pl.BlockSpec((tm, C_WIDTH), row),
            pl.BlockSpec((A_GROUPS, BLOCK, BLOCK), lambda i: (0, 0, 0)),
            pl.BlockSpec((BLOCK, A_GROUPS), const2),
            pl.BlockSpec((1, A_WIDTH), const2),
            pl.BlockSpec((A_WIDTH, d), const2),
            pl.BlockSpec((A_WIDTH, d), const2),
            pl.BlockSpec((C_WIDTH, d), const2),
            pl.BlockSpec((d, d), const2),
            pl.BlockSpec((1, d), const2),
            pl.BlockSpec((N_EXPERTS, d), const2),
            pl.BlockSpec((N_EXPERTS, 1), const2),
        ],
        out_specs=[
            pl.BlockSpec((tm, d), row),
            pl.BlockSpec((tm, XE_WIDTH), row),
            pl.BlockSpec((1, tm), lambda i: (0, i)),
            pl.BlockSpec((1, tm), lambda i: (0, i)),
            pl.BlockSpec((8, LANES), const2),
        ],
        out_shape=[
            jax.ShapeDtypeStruct((t, d), F32),
            jax.ShapeDtypeStruct((t, XE_WIDTH), F32),
            jax.ShapeDtypeStruct((1, t), jnp.int32),
            jax.ShapeDtypeStruct((1, t), jnp.int32),
            jax.ShapeDtypeStruct((8, LANES), F32),
        ],
        scratch_shapes=[pltpu.VMEM((tm, A_WIDTH), BF16), pltpu.VMEM((8, LANES), F32)],
        compiler_params=pltpu.CompilerParams(
            dimension_semantics=("arbitrary",), vmem_limit_bytes=VMEM_LIMIT),
        name="merge",
    )(x2, proj, proj, proj, proj, proj, yb, yc, w_s, b_s_t, g_v.reshape(1, A_WIDTH),
      wa, wb, wc, wo, norm_g.reshape(1, d), router_wt, router_b.reshape(N_EXPERTS, 1))


def _route_tables(cnt, grp, rank, t, rt):
    cnt = cnt[:N_GROUPS, 0].astype(jnp.int32)
    padded = (cnt + rt - 1) // rt * rt
    off_end = jnp.cumsum(padded)
    off = off_end - padded
    g = grp.reshape(t)
    pos = rank.reshape(t)
    for k in range(N_GROUPS):
        pos = pos + jnp.where(g == k, off[k], 0)
    starts = jnp.arange(t // rt + N_GROUPS, dtype=jnp.int32) * rt
    tile_group = jnp.minimum(jnp.sum((starts[:, None] >= off_end[None, :]).astype(jnp.int32), axis=1),
                             N_GROUPS - 1)
    tile_valid = (starts < off_end[N_GROUPS - 1]).astype(jnp.int32)
    last_tile = jnp.maximum(off_end[N_GROUPS - 1] // rt - 1, 0).reshape(1)
    tail = off_end[N_GROUPS - 1] + jnp.arange(N_GROUPS, dtype=jnp.int32) * rt
    fill_row = jnp.concatenate([jnp.maximum(off_end - rt, 0), jnp.minimum(tail, t + (N_GROUPS - 1) * rt)])
    fill_on = jnp.concatenate([padded > 0, tail < t + N_GROUPS * rt]).astype(jnp.int32)
    return pos, tile_group, tile_valid, last_tile, fill_row, fill_on


ROW_DMA_UNROLL = 8


def _row_copy(src_ref, src_row, dst_ref, dst_row, sem):
    return pltpu.make_async_copy(src_ref.at[pl.ds(src_row, 1)], dst_ref.at[pl.ds(dst_row, 1)], sem)


def _dispatch_kernel(pos_ref, fill_row_ref, fill_on_ref, xe_ref, xs_ref, zero_ref, sem, *, tm, rt):
    base = pl.program_id(0) * tm

    @pl.when(pl.program_id(0) == 0)
    def _():
        zero_ref[...] = jnp.zeros(zero_ref.shape, F32)
        for k in range(2 * N_GROUPS):
            @pl.when(fill_on_ref[k] == 1)
            def _():
                row = pl.multiple_of(fill_row_ref[k], rt)
                fill = pltpu.make_async_copy(zero_ref, xs_ref.at[pl.ds(row, rt)], sem)
                fill.start()
                fill.wait()

    def issue(r, c):
        _row_copy(xe_ref, r, xs_ref, pos_ref[base + r], sem).start()
        return c

    lax.fori_loop(0, tm, issue, 0, unroll=ROW_DMA_UNROLL)
    pltpu.make_async_copy(xe_ref, xs_ref.at[pl.ds(0, tm)], sem).wait()


def moe_dispatch(pos, fill_row, fill_on, xe, t_pad, rt, tm=512):
    t, w = xe.shape
    tm = min(tm, t)
    kern = functools.partial(_dispatch_kernel, tm=tm, rt=rt)
    return pl.pallas_call(
        kern,
        grid_spec=pltpu.PrefetchScalarGridSpec(
            num_scalar_prefetch=3,
            grid=(t // tm,),
            in_specs=[pl.BlockSpec((tm, w), lambda i, *_: (i, 0))],
            out_specs=pl.BlockSpec(memory_space=pl.ANY),
            scratch_shapes=[pltpu.VMEM((rt, w), F32), pltpu.SemaphoreType.DMA],
        ),
        out_shape=jax.ShapeDtypeStruct((t_pad, w), F32),
        compiler_params=pltpu.CompilerParams(
            dimension_semantics=("arbitrary",), vmem_limit_bytes=VMEM_LIMIT),
        name="moe_dispatch",
    )(pos, fill_row, fill_on, xe)


def _moe_ffn_kernel(tg_ref, tv_ref, lt_ref, xs_ref, w1_ref, w3_ref, w2_ref, ys_ref):
    del lt_ref
    i = pl.program_id(0)
    valid = tv_ref[i] == 1

    @pl.when(valid)
    def _():
        x = xs_ref[:, :D_MODEL].astype(BF16)
        ext = xs_ref[:, D_MODEL:]
        lane = lax.broadcasted_iota(jnp.int32, ext.shape, 1)
        first = tg_ref[i] * EXPERTS_PER_GROUP

        def up(e):
            return _dot(x, w1_ref[e]), _dot(x, w3_ref[e])

        def down(e, ab):
            gcol = jnp.sum(jnp.where(lane == first + e, ext, 0.0), axis=-1, keepdims=True)
            hh = ((jax.nn.silu(ab[0]) * ab[1]) * gcol).astype(BF16)
            return _dot(hh, w2_ref[e])

        ab = up(0)
        out = None
        for e in range(EXPERTS_PER_GROUP):
            nxt = up(e + 1) if e + 1 < EXPERTS_PER_GROUP else None
            o = down(e, ab)
            out = o if out is None else out + o
            ab = nxt
        ys_ref[...] = out

    @pl.when(jnp.logical_not(valid))
    def _():
        ys_ref[...] = jnp.zeros(ys_ref.shape, F32)


def moe_ffn(tile_group, tile_valid, last_tile, xs, w1, w3, w2, rt):
    t_pad, w = xs.shape
    d = D_MODEL
    widx = lambda i, tg, tv, lt: (tg[i], 0, 0)
    return pl.pallas_call(
        _moe_ffn_kernel,
        grid_spec=pltpu.PrefetchScalarGridSpec(
            num_scalar_prefetch=3,
            grid=(t_pad // rt,),
            in_specs=[pl.BlockSpec((rt, w), lambda i, tg, tv, lt: (jnp.minimum(i, lt[0]), 0)),
                      pl.BlockSpec((EXPERTS_PER_GROUP, d, D_FF_EXPERT), widx),
                      pl.BlockSpec((EXPERTS_PER_GROUP, d, D_FF_EXPERT), widx),
                      pl.BlockSpec((EXPERTS_PER_GROUP, D_FF_EXPERT, d), widx)],
            out_specs=pl.BlockSpec((rt, d), lambda i, tg, tv, lt: (i, 0)),
        ),
        out_shape=jax.ShapeDtypeStruct((t_pad, d), F32),
        compiler_params=pltpu.CompilerParams(
            dimension_semantics=("arbitrary",), vmem_limit_bytes=VMEM_LIMIT),
        name="moe_ffn",
    )(tile_group, tile_valid, last_tile, xs, w1, w3, w2)


def _combine_kernel(pos_ref, h_ref, ys_ref, fg_ref, o_ref, ybuf_ref, sem, *, tm, final_norm):
    base = pl.program_id(0) * tm

    def issue(r, c):
        _row_copy(ys_ref, pos_ref[base + r], ybuf_ref, r, sem).start()
        return c

    lax.fori_loop(0, tm, issue, 0, unroll=ROW_DMA_UNROLL)
    pltpu.make_async_copy(ys_ref.at[pl.ds(0, tm)], ybuf_ref, sem).wait()
    y = h_ref[...] + ybuf_ref[...]
    if final_norm:
        y = _rms(y, fg_ref[...])
    o_ref[...] = y


def moe_combine(pos, h, ys, final_gain, final_norm, tm=512):
    t, d = h.shape
    tm = min(tm, t)
    kern = functools.partial(_combine_kernel, tm=tm, final_norm=final_norm)
    return pl.pallas_call(
        kern,
        grid_spec=pltpu.PrefetchScalarGridSpec(
            num_scalar_prefetch=1,
            grid=(t // tm,),
            in_specs=[pl.BlockSpec((tm, d), lambda i, pos: (i, 0)),
                      pl.BlockSpec(memory_space=pl.ANY),
                      pl.BlockSpec((1, d), lambda i, pos: (0, 0))],
            out_specs=pl.BlockSpec((tm, d), lambda i, pos: (i, 0)),
            scratch_shapes=[pltpu.VMEM((tm, d), F32), pltpu.SemaphoreType.DMA],
        ),
        out_shape=jax.ShapeDtypeStruct((t, d), F32),
        compiler_params=pltpu.CompilerParams(
            dimension_semantics=("arbitrary",), vmem_limit_bytes=VMEM_LIMIT),
        name="moe_combine",
    )(pos, h, ys, final_gain.reshape(1, d))


def grouped_moe(h, xe, grp, rank, cnt, w1, w3, w2, final_gain, final_norm, rt=512):
    t = h.shape[0]
    rt = min(rt, t)
    pos, tile_group, tile_valid, last_tile, fill_row, fill_on = _route_tables(cnt, grp, rank, t, rt)
    xs = moe_dispatch(pos, fill_row, fill_on, xe, t + N_GROUPS * rt, rt)
    ys = moe_ffn(tile_group, tile_valid, last_tile, xs, w1, w3, w2, rt)
    return moe_combine(pos, h, ys, final_gain, final_norm)


def kernel(x, norm_mix_gain, w_in, gmlp_w_s, gmlp_b_s, gmlp_v_gain, diff_lambda, diff_subln_gain,
           w_up_a, w_up_b, w_up_c, w_out, norm_ffn_gain, router_w, router_bias,
           moe_w1, moe_w3, moe_w2, final_gain):
    bsz, seq, d = x.shape
    depth = w_in.shape[0]
    x2 = x.reshape(bsz * seq, d)
    router_wt = router_w.T
    for l in range(depth):
        lambda_init = 0.8 - 0.6 * math.exp(-0.3 * l)
        proj = in_proj(x2, norm_mix_gain[l], w_in[l].astype(BF16))
        yb = diff_attn(proj, diff_lambda[l], diff_subln_gain[l], bsz, seq, lambda_init)
        yc = stick_attn(proj, bsz, seq)
        h, xe, grp, rank, cnt = merge(x2, proj, yb, yc, gmlp_w_s[l], gmlp_b_s[l].T, gmlp_v_gain[l],
                                      w_up_a[l].astype(BF16), w_up_b[l].astype(BF16), w_up_c[l].astype(BF16),
                                      w_out[l].astype(BF16), norm_ffn_gain[l], router_wt, router_bias)
        x2 = grouped_moe(h, xe, grp, rank, cnt, moe_w1[l].astype(BF16), moe_w3[l].astype(BF16),
                         moe_w2[l].astype(BF16), final_gain, final_norm=(l == depth - 1))
    return x2.reshape(bsz, seq, d)
```

```python
import functools
import math

import jax
import jax.numpy as jnp
from jax import lax
from jax.experimental import pallas as pl
from jax.experimental.pallas import tpu as pltpu

D_MODEL = 1024
BLOCK = 128
A_GROUPS = 4
A_WIDTH = 512
B_HEADS = 4
B_QK_DIM = 64
B_V_DIM = 128
C_HEADS = 8
C_HEAD_DIM = 64
C_WIDTH = 512
N_EXPERTS = 16
N_GROUPS = 4
EXPERTS_PER_GROUP = 4
D_FF_EXPERT = 512
IN_WIDTH = 7168
XE_WIDTH = D_MODEL + 128
EPS = 1e-6

LANES = 128
VMEM_LIMIT = 56 * 1024 * 1024
F32 = jnp.float32
BF16 = jnp.bfloat16
NEG_BIG = -1e30
EXP_UNDERFLOW = -104.0
EXP2_UNDERFLOW = -152.0

LOG2E = 1.4426950408889634

COL_QB, COL_KB, COL_VB = 8, 12, 16
QB_COLS = (2 * A_WIDTH, 2 * A_WIDTH + 512)
QC_COLS = (2 * A_WIDTH + 3 * 512, 2 * A_WIDTH + 4 * 512)


def _nt_dot(a, b):
    return lax.dot_general(a, b, (((1,), (1,)), ((), ())), preferred_element_type=F32)


def _dot(a, b):
    return jnp.dot(a, b, preferred_element_type=F32)


def _sigmoid(x):
    return 0.5 * jnp.tanh(0.5 * x) + 0.5


def _rms(xf, gain):
    return xf * lax.rsqrt(jnp.mean(xf * xf, axis=-1, keepdims=True) + EPS) * gain


def _in_proj_kernel(x_ref, g_ref, w_ref, cs_ref, o_ref, xn_ref):
    @pl.when(pl.program_id(1) == 0)
    def _():
        xn_ref[...] = _rms(x_ref[...], g_ref[...]).astype(BF16)

    o_ref[...] = (_dot(xn_ref[...], w_ref[...]) * cs_ref[...]).astype(BF16)


def _query_scale_row(n):
    col = jnp.arange(n)
    s = jnp.ones((n,), F32)
    s = jnp.where((col >= QB_COLS[0]) & (col < QB_COLS[1]), LOG2E * B_QK_DIM ** -0.5, s)
    s = jnp.where((col >= QC_COLS[0]) & (col < QC_COLS[1]), C_HEAD_DIM ** -0.5, s)
    return s.reshape(1, n)


def in_proj(x2, gain, w_all, layer, tm=1024, tn=1792):
    t, d = x2.shape
    n = w_all.shape[2]
    tm = min(tm, t)
    return pl.pallas_call(
        _in_proj_kernel,
        grid=(t // tm, n // tn),
        in_specs=[
            pl.BlockSpec((tm, d), lambda i, j: (i, 0)),
            pl.BlockSpec((1, d), lambda i, j: (0, 0)),
            pl.BlockSpec((None, d, tn), lambda i, j: (layer, 0, j)),
            pl.BlockSpec((1, tn), lambda i, j: (0, j)),
        ],
        out_specs=pl.BlockSpec((tm, tn), lambda i, j: (i, j)),
        out_shape=jax.ShapeDtypeStruct((t, n), BF16),
        scratch_shapes=[pltpu.VMEM((tm, d), BF16)],
        compiler_params=pltpu.CompilerParams(
            dimension_semantics=("arbitrary", "arbitrary"), vmem_limit_bytes=VMEM_LIMIT),
        name="in_proj",
    )(x2, gain.reshape(1, d), w_all, _query_scale_row(n))


def _diff_attn_kernel(lam_ref, sg_ref, q_ref, k_ref, v_ref, o_ref, m_ref, l_ref, acc_ref, s_ref, p_ref, kmax_ref,
                      j0_ref,
                      *, tq, hps, nq, lambda_init):
    tk = tq
    hp = pl.program_id(1)
    qi = pl.program_id(2)
    nc = tk // LANES
    units = [(hh, mi) for hh in range(hps) for mi in range(2)]
    head_cols = [slice(hh * LANES, (hh + 1) * LANES) for hh in range(hps)]

    def head_slope(h):
        sl = jnp.where(h == 0, 0.25, jnp.where(h == 1, 0.0625, jnp.where(h == 2, 0.015625, 0.00390625)))
        return sl.astype(F32) * LOG2E

    slopes = [head_slope(hp * hps + hh) for hh in range(hps)]

    lane = lax.broadcasted_iota(jnp.int32, (tq, LANES), 1)
    qs = []
    for hh, mi in units:
        q = q_ref[:, head_cols[hh]]
        keep = (lane < B_QK_DIM) if mi == 0 else (lane >= B_QK_DIM)
        qs.append(jnp.where(keep, q, jnp.zeros_like(q)))

    m_ref[...] = jnp.full(m_ref.shape, NEG_BIG, F32)
    l_ref[...] = jnp.zeros(l_ref.shape, F32)
    acc_ref[...] = jnp.zeros(acc_ref.shape, F32)

    kcol = lax.broadcasted_iota(jnp.int32, (1, LANES), 1)
    row_minus_col = lax.broadcasted_iota(jnp.int32, (tq, LANES), 0) - lane
    q0 = qi * tq

    def load(ref, kb, hh):
        return ref[pl.ds(pl.multiple_of(kb * tk, tk), tk), head_cols[hh]]

    def scores(kb):
        kks = [load(k_ref, kb, hh) for hh in range(hps)]
        return [_nt_dot(qs[u], kks[hh]) for u, (hh, _) in enumerate(units)]

    def softmax_step(u, s, kb, masked):
        off = kb * tk - q0
        slope = slopes[units[u][0]]
        cols = []
        for c in range(nc):
            sc = s[:, c * LANES:(c + 1) * LANES] + slope * (kcol + (off + c * LANES)).astype(F32)
            if masked:
                sc = jnp.where(row_minus_col >= (off + c * LANES), sc, NEG_BIG)
            cols.append(sc)
        mx = cols[0]
        for sc in cols[1:]:
            mx = jnp.maximum(mx, sc)
        m_prev = m_ref[u]
        m_new = jnp.maximum(m_prev, jnp.max(mx, axis=-1, keepdims=True))
        alpha = jnp.exp2(m_prev - m_new)
        ps = [jnp.exp2(sc - m_new) for sc in cols]
        psum = ps[0]
        for pc in ps[1:]:
            psum = psum + pc
        l_ref[u] = alpha * l_ref[u] + psum
        acc_ref[u] = alpha * acc_ref[u]
        m_ref[u] = m_new
        return jnp.concatenate([pc.astype(BF16) for pc in ps], axis=1)

    lanes1 = lax.broadcasted_iota(jnp.int32, (1, LANES), 1)

    hs_i = lax.broadcasted_iota(jnp.int32, (2 * LANES, LANES), 0) % LANES
    hs_j = lax.broadcasted_iota(jnp.int32, (2 * LANES, LANES), 1)
    half_sum = jnp.where((hs_i < B_QK_DIM) == (hs_j < B_QK_DIM), 1.0, 0.0).astype(BF16)

    def half_sums(x):
        hi = x.astype(BF16)
        lo = (x - hi.astype(F32)).astype(BF16)
        return _dot(jnp.concatenate([hi, lo], axis=1), half_sum)

    steepest = slopes[0]
    for sl in slopes[1:]:
        steepest = jnp.maximum(steepest, sl)
    far_enough = lambda first_row: steepest * (first_row - tk + 1).astype(F32) > -EXP2_UNDERFLOW

    @pl.when(jnp.logical_and(qi == 0, far_enough(jnp.int32((nq - 1) * tq))))
    def _():
        for hh in range(hps):
            rows = [jnp.zeros((1, LANES), F32) for _ in range(2)]
            for kb in range(nq):
                kk = k_ref[kb * tk:(kb + 1) * tk, head_cols[hh]].astype(F32)
                n2 = jnp.sqrt(jnp.max(half_sums(kk * kk), axis=0, keepdims=True))
                for mi in range(2):
                    at = mi * B_QK_DIM
                    rows[mi] = jnp.where(lanes1 == kb, n2[:, at:at + 1], rows[mi])
            for mi in range(2):
                kmax_ref[2 * hh + mi, 0:1, :] = rows[mi]

    j0_ref[0] = 0

    @pl.when(far_enough(q0))
    def _():
        worst = None
        for hh in range(hps):
            qf = q_ref[:, head_cols[hh]].astype(F32)
            k_own = k_ref[pl.ds(pl.multiple_of(q0, tq), tq), head_cols[hh]].astype(F32)
            score_min = jnp.min(half_sums(qf * k_own), axis=0, keepdims=True)
            norm_max = jnp.sqrt(jnp.max(half_sums(qf * qf), axis=0, keepdims=True))
            top_bias = slopes[hh] * ((lanes1 + 1) * tk - 1 - q0).astype(F32)
            for mi in range(2):
                u = 2 * hh + mi
                at = mi * B_QK_DIM
                gap = norm_max[:, at:at + 1] * kmax_ref[u, 0:1, :] + top_bias - score_min[:, at:at + 1]
                worst = gap if worst is None else jnp.maximum(worst, gap)
        needed = jnp.logical_or(worst >= EXP2_UNDERFLOW, lanes1 >= qi)
        j0_ref[0] = jnp.min(jnp.where(needed, lanes1, LANES))

    j0 = j0_ref[0]

    odd = lax.rem(qi - j0, 2)
    s0 = scores(j0)
    for u in range(len(units)):
        s_ref[odd, u] = s0[u]
        p_ref[odd, u] = jnp.zeros((tq, tk), BF16)

    def pending_pv(j, slot):
        vs = [load(v_ref, jnp.maximum(j - 1, 0), hh) for hh in range(hps)]
        return [_dot(p_ref[slot, u], vs[hh]) for u, (hh, _) in enumerate(units)]

    def step(j, slot):
        nxt = 1 - slot
        pv = pending_pv(j, slot)
        sn = scores(j + 1)
        for u in range(len(units)):
            acc_ref[u] += pv[u]
            s_ref[nxt, u] = sn[u]
        for u in range(len(units)):
            p_ref[nxt, u] = softmax_step(u, s_ref[slot, u], j, False)

    @pl.when(odd == 1)
    def _():
        step(j0, 1)

    def body(t, c):
        j = j0 + odd + 2 * t
        step(j, 0)
        step(j + 1, 1)
        return c

    lax.fori_loop(0, (qi - j0) // 2, body, 0)
    pv = pending_pv(qi, 0)
    for u in range(len(units)):
        acc_ref[u] += pv[u]
    vdiag = [load(v_ref, qi, hh) for hh in range(hps)]
    for u, (hh, _) in enumerate(units):
        p = softmax_step(u, s_ref[0, u], qi, True)
        acc_ref[u] += _dot(p, vdiag[hh])

    lp = lam_ref[...]
    s1 = jnp.sum(lp[0:1] * lp[1:2], axis=-1, keepdims=True)
    s2 = jnp.sum(lp[2:3] * lp[3:4], axis=-1, keepdims=True)
    lam = jnp.exp(s1) - jnp.exp(s2) + lambda_init
    for hh in range(hps):
        l0 = jnp.sum(l_ref[2 * hh], axis=-1, keepdims=True)
        l1 = jnp.sum(l_ref[2 * hh + 1], axis=-1, keepdims=True)
        o = acc_ref[2 * hh] / l0 - lam * (acc_ref[2 * hh + 1] / l1)
        o_ref[:, head_cols[hh]] = (_rms(o, sg_ref[...]) * (1.0 - lambda_init)).astype(BF16)


def diff_attn(proj, lam_params, subln_g, bsz, seq, lambda_init, tq=512, hps=1):
    tq = min(tq, seq)
    nq = seq // tq
    n_units = 2 * hps
    w = hps * LANES
    kern = functools.partial(_diff_attn_kernel, tq=tq, hps=hps, nq=nq, lambda_init=lambda_init)
    return pl.pallas_call(
        kern,
        grid=(bsz, B_HEADS // hps, nq),
        in_specs=[
            pl.BlockSpec((4, B_QK_DIM), lambda b, h, i: (0, 0)),
            pl.BlockSpec((1, B_V_DIM), lambda b, h, i: (0, 0)),
            pl.BlockSpec((tq, w), lambda b, h, i: (b * nq + i, COL_QB // hps + h)),
            pl.BlockSpec((seq, w), lambda b, h, i: (b, COL_KB // hps + h)),
            pl.BlockSpec((seq, w), lambda b, h, i: (b, COL_VB // hps + h)),
        ],
        out_specs=pl.BlockSpec((tq, w), lambda b, h, i: (b * nq + i, h)),
        out_shape=jax.ShapeDtypeStruct((bsz * seq, B_HEADS * B_V_DIM), BF16),
        scratch_shapes=[pltpu.VMEM((n_units, tq, LANES), F32), pltpu.VMEM((n_units, tq, LANES), F32),
                        pltpu.VMEM((n_units, tq, B_V_DIM), F32),
                        pltpu.VMEM((2, n_units, tq, tq), F32), pltpu.VMEM((2, n_units, tq, tq), BF16),
                        pltpu.VMEM((n_units, 8, LANES), F32), pltpu.SMEM((1,), jnp.int32)],
        compiler_params=pltpu.CompilerParams(
            dimension_semantics=("arbitrary", "arbitrary", "arbitrary"), vmem_limit_bytes=VMEM_LIMIT),
        name="diff_attn",
    )(lam_params, subln_g.reshape(1, B_V_DIM), proj, proj, proj)


def _stick_kernel(q_ref, k_ref, v_ref, o_ref, qs_ref, carry_ref, acc_ref, *, tq, n_chains):
    tk = tq
    n_pairs = C_HEADS // 2
    gi = pl.program_id(1)
    lane = lax.broadcasted_iota(jnp.int32, (tq, LANES), 1)
    low = lane < C_HEAD_DIM
    units = [(c, p) for c in range(n_chains) for p in range(n_pairs)]
    for c, p in units:
        q = q_ref[c * tq:(c + 1) * tq, p * LANES:(p + 1) * LANES]
        zero = jnp.zeros_like(q)
        qs_ref[c, p, 0:tq] = jnp.where(low, q, zero)
        qs_ref[c, p, tq:2 * tq] = jnp.where(low, zero, q)
    carry_ref[...] = jnp.zeros(carry_ref.shape, F32)
    acc_ref[...] = jnp.zeros(acc_ref.shape, F32)

    row = lax.broadcasted_iota(jnp.int32, (2 * tq, tk), 0)
    col = lax.broadcasted_iota(jnp.int32, (2 * tq, tk), 1)
    strict = (row % tq) > col
    uj = lax.broadcasted_iota(jnp.int32, (2 * tk, tk + LANES), 0) % tk
    us = lax.broadcasted_iota(jnp.int32, (2 * tk, tk + LANES), 1)
    w2 = jnp.where(jnp.logical_or(uj > us, us >= tk), 1.0, 0.0).astype(BF16)
    vlow = lax.broadcasted_iota(jnp.int32, (tk, LANES), 1) < C_HEAD_DIM

    def step(n, masked):
        kbs = [gi * n_chains + c - n for c in range(n_chains)]
        starts = [pl.multiple_of(jnp.maximum(kb, 0) * tk, tk) for kb in kbs]
        gone = [jnp.where(kb < 0, NEG_BIG, 0.0).astype(F32) for kb in kbs]
        cols = [slice(p * LANES, (p + 1) * LANES) for p in range(n_pairs)]
        zs = [_nt_dot(qs_ref[c, p], k_ref[pl.ds(starts[c], tk), cols[p]]) for c, p in units]
        lbs, rs = [], []
        for z in zs:
            sp = jnp.log(1.0 + jnp.exp(-jnp.abs(z)))
            lb = jnp.minimum(z, 0.0) - sp
            lk = lb - z
            if masked:
                lk = jnp.where(strict, lk, 0.0)
            hi = lk.astype(BF16)
            lo = (lk - hi.astype(F32)).astype(BF16)
            lbs.append(lb)
            rs.append(_dot(jnp.concatenate([hi, lo], axis=1), w2))
        alive = None
        for u, (c, p) in enumerate(units):
            carry = carry_ref[c, p]
            a = jnp.exp(lbs[u] + rs[u][:, :tk] + (carry + gone[c]))
            if masked:
                a = jnp.where(strict, a, 0.0)
            carry = carry + rs[u][:, tk:]
            carry_ref[c, p] = carry
            left = carry + jnp.where(kbs[c] < 1, NEG_BIG, 0.0).astype(F32)
            alive = left if alive is None else jnp.maximum(alive, left)
            ab = a.astype(BF16)
            vv = v_ref[pl.ds(starts[c], tk), cols[p]]
            zero = jnp.zeros_like(vv)
            vst = jnp.concatenate([jnp.where(vlow, vv, zero), jnp.where(vlow, zero, vv)], axis=0)
            acc_ref[c, p] += _dot(jnp.concatenate([ab[:tq], ab[tq:]], axis=1), vst)
        return jnp.max(alive)

    def cond(st):
        _, alive = st
        return alive > EXP_UNDERFLOW

    def body(st):
        n, _ = st
        return n + 1, step(n, False)

    lax.while_loop(cond, body, (1, step(0, True)))
    for c, p in units:
        o_ref[c * tq:(c + 1) * tq, p * LANES:(p + 1) * LANES] = acc_ref[c, p].astype(BF16)


def stick_attn(proj, bsz, seq, tq=128, n_chains=2):
    n_chains = min(n_chains, seq // tq)
    rows = tq * n_chains
    nq = seq // rows
    kern = functools.partial(_stick_kernel, tq=tq, n_chains=n_chains)
    n_pairs = C_HEADS // 2
    return pl.pallas_call(
        kern,
        grid=(bsz, nq),
        in_specs=[
            pl.BlockSpec((rows, C_WIDTH), lambda b, i: (b * nq + i, 5)),
            pl.BlockSpec((seq, C_WIDTH), lambda b, i: (b, 6)),
            pl.BlockSpec((seq, C_WIDTH), lambda b, i: (b, 7)),
        ],
        out_specs=pl.BlockSpec((rows, C_WIDTH), lambda b, i: (b * nq + i, 0)),
        out_shape=jax.ShapeDtypeStruct((bsz * seq, C_WIDTH), BF16),
        scratch_shapes=[pltpu.VMEM((n_chains, n_pairs, 2 * tq, LANES), BF16),
                        pltpu.VMEM((n_chains, n_pairs, 2 * tq, LANES), F32),
                        pltpu.VMEM((n_chains, n_pairs, tq, LANES), F32)],
        compiler_params=pltpu.CompilerParams(
            dimension_semantics=("arbitrary", "arbitrary"), vmem_limit_bytes=VMEM_LIMIT),
        name="stick_attn",
    )(proj, proj, proj)


def _top2_route(sel, aff):
    def top2_sum(a, b, c, d):
        hi1, lo1 = jnp.maximum(a, b), jnp.minimum(a, b)
        hi2, lo2 = jnp.maximum(c, d), jnp.minimum(c, d)
        return jnp.maximum(hi1, hi2) + jnp.maximum(jnp.minimum(hi1, hi2), jnp.maximum(lo1, lo2))

    scores = [top2_sum(*sel[4 * g:4 * g + 4]) for g in range(N_GROUPS)]
    best = jnp.zeros_like(scores[0], dtype=jnp.int32)
    best_score = scores[0]
    for g in range(1, N_GROUPS):
        better = scores[g] > best_score
        best = jnp.where(better, g, best)
        best_score = jnp.where(better, scores[g], best_score)

    def pick(rows, i):
        out = rows[i]
        for g in range(1, N_GROUPS):
            out = jnp.where(best == g, rows[4 * g + i], out)
        return out

    cs = [pick(sel, i) for i in range(EXPERTS_PER_GROUP)]
    ca = [pick(aff, i) for i in range(EXPERTS_PER_GROUP)]

    def argmax_first(vals, exclude=None):
        idx = jnp.full(vals[0].shape, -1, jnp.int32)
        cur = jnp.full(vals[0].shape, -jnp.inf, F32)
        for i, v in enumerate(vals):
            ok = v > cur
            if exclude is not None:
                ok = jnp.logical_and(ok, exclude != i)
            idx = jnp.where(ok, i, idx)
            cur = jnp.where(ok, v, cur)
        return idx

    i1 = argmax_first(cs)
    i2 = argmax_first(cs, exclude=i1)

    def take(vals, idx):
        out = vals[0]
        for i in range(1, len(vals)):
            out = jnp.where(idx == i, vals[i], out)
        return out

    w1, w2 = take(ca, i1), take(ca, i2)
    tot = w1 + w2
    w1, w2 = w1 / tot, w2 / tot
    e1 = best * EXPERTS_PER_GROUP + i1
    e2 = best * EXPERTS_PER_GROUP + i2
    gates = [jnp.where(e1 == e, w1, 0.0) + jnp.where(e2 == e, w2, 0.0) for e in range(N_EXPERTS)]
    return gates, best


def _merge_kernel(x_ref, u_ref, v_ref, g0_ref, g1_ref, g2_ref, yb_ref, yc_ref,
                  ws_ref, bst_ref, gv_ref, wa_ref, wb_ref, wc_ref, wo_ref, ng_ref, rwt_ref, rb_ref,
                  h_ref, xe_ref, grp_ref, rank_ref, cnt_ref, ya_ref, run_ref, *, tm):
    u = jax.nn.gelu(u_ref[...].astype(F32))
    v = _rms(jax.nn.gelu(v_ref[...].astype(F32)), gv_ref[...]).astype(BF16)

    r = lax.broadcasted_iota(jnp.int32, (BLOCK, BLOCK), 0)
    c = lax.broadcasted_iota(jnp.int32, (BLOCK, BLOCK), 1)
    causal = r >= c
    for g in range(A_GROUPS):
        w = jnp.where(causal, ws_ref[g], 0.0).astype(BF16)
        bias = bst_ref[:, g:g + 1]
        cols = slice(g * LANES, (g + 1) * LANES)
        for ch in range(tm // BLOCK):
            rows = slice(ch * BLOCK, (ch + 1) * BLOCK)
            mixed = _dot(w, v[rows, cols]) + bias
            ya_ref[rows, cols] = (u[rows, cols] * mixed).astype(BF16)

    merged = _sigmoid(g0_ref[...].astype(F32)) * _dot(ya_ref[...], wa_ref[...])
    merged += _sigmoid(g1_ref[...].astype(F32)) * _dot(yb_ref[...], wb_ref[...])
    merged += _sigmoid(g2_ref[...].astype(F32)) * _dot(yc_ref[...], wc_ref[...])
    h = x_ref[...] + _dot(merged.astype(BF16), wo_ref[...])
    h_ref[...] = h

    xn = _rms(h, ng_ref[...])
    xe_ref[:, :D_MODEL] = xn

    xh = xn.astype(BF16)
    xl = (xn - xh.astype(F32)).astype(BF16)
    rw = rwt_ref[...]
    rh = rw.astype(BF16)
    rl = (rw - rh.astype(F32)).astype(BF16)
    logits = _nt_dot(rh, xh) + (_nt_dot(rh, xl) + _nt_dot(rl, xh))
    aff = jax.nn.sigmoid(logits)
    sel = aff + rb_ref[...]
    gates, best = _top2_route([sel[e:e + 1] for e in range(N_EXPERTS)],
                              [aff[e:e + 1] for e in range(N_EXPERTS)])
    gate_rows = jnp.concatenate(gates + [jnp.zeros((LANES - N_EXPERTS, tm), F32)], axis=0)
    xe_ref[:, D_MODEL:] = gate_rows.T

    @pl.when(pl.program_id(0) == 0)
    def _():
        run_ref[...] = jnp.zeros(run_ref.shape, F32)

    gid = lax.broadcasted_iota(jnp.int32, (8, tm), 0)
    member = (gid == best).astype(F32)
    before = lax.broadcasted_iota(jnp.int32, (tm, tm), 0) < lax.broadcasted_iota(jnp.int32, (tm, tm), 1)
    prefix = _dot(member.astype(BF16), before.astype(BF16))
    run = run_ref[...]
    rank = jnp.sum(member * (prefix + run[:, 0:1]), axis=0, keepdims=True)
    rank_ref[...] = rank.astype(jnp.int32)
    grp_ref[...] = best
    run = run + jnp.sum(member, axis=-1, keepdims=True)
    run_ref[...] = run
    cnt_ref[...] = run


def merge(x2, proj, yb, yc, w_s, b_s_t, g_v, wa, wb, wc, wo, layer, norm_g, router_wt, router_b, tm=512):
    t, d = x2.shape
    tm = min(tm, t)
    row = lambda i: (i, 0)
    const2 = lambda i: (0, 0)
    of_layer = lambda i: (layer, 0, 0)
    kern = functools.partial(_merge_kernel, tm=tm)
    return pl.pallas_call(
        kern,
        grid=(t // tm,),
        in_specs=[
            pl.BlockSpec((tm, d), row),
            pl.BlockSpec((tm, A_WIDTH), lambda i: (i, 0)),
            pl.BlockSpec((tm, A_WIDTH), lambda i: (i, 1)),
            pl.BlockSpec((tm, d), lambda i: (i, 4)),
            pl.BlockSpec((tm, d), lambda i: (i, 5)),
            pl.BlockSpec((tm, d), lambda i: (i, 6)),
            pl.BlockSpec((tm, A_WIDTH), row),
            pl.BlockSpec((tm, C_WIDTH), row),
            pl.BlockSpec((A_GROUPS, BLOCK, BLOCK), lambda i: (0, 0, 0)),
            pl.BlockSpec((BLOCK, A_GROUPS), const2),
            pl.BlockSpec((1, A_WIDTH), const2),
            pl.BlockSpec((None, A_WIDTH, d), of_layer),
            pl.BlockSpec((None, A_WIDTH, d), of_layer),
            pl.BlockSpec((None, C_WIDTH, d), of_layer),
            pl.BlockSpec((None, d, d), of_layer),
            pl.BlockSpec((1, d), const2),
            pl.BlockSpec((N_EXPERTS, d), const2),
            pl.BlockSpec((N_EXPERTS, 1), const2),
        ],
        out_specs=[
            pl.BlockSpec((tm, d), row),
            pl.BlockSpec((tm, XE_WIDTH), row),
            pl.BlockSpec((1, tm), lambda i: (0, i)),
            pl.BlockSpec((1, tm), lambda i: (0, i)),
            pl.BlockSpec((8, LANES), const2),
        ],
        out_shape=[
            jax.ShapeDtypeStruct((t, d), F32),
            jax.ShapeDtypeStruct((t, XE_WIDTH), F32),
            jax.ShapeDtypeStruct((1, t), jnp.int32),
            jax.ShapeDtypeStruct((1, t), jnp.int32),
            jax.ShapeDtypeStruct((8, LANES), F32),
        ],
        scratch_shapes=[pltpu.VMEM((tm, A_WIDTH), BF16), pltpu.VMEM((8, LANES), F32)],
        compiler_params=pltpu.CompilerParams(
            dimension_semantics=("arbitrary",), vmem_limit_bytes=VMEM_LIMIT),
        name="merge",
    )(x2, proj, proj, proj, proj, proj, yb, yc, w_s, b_s_t, g_v.reshape(1, A_WIDTH),
      wa, wb, wc, wo, norm_g.reshape(1, d), router_wt, router_b.reshape(N_EXPERTS, 1))


def _route_tables(cnt, grp, rank, t, rt):
    cnt = cnt[:N_GROUPS, 0].astype(jnp.int32)
    padded = (cnt + rt - 1) // rt * rt
    off_end = jnp.cumsum(padded)
    off = off_end - padded
    g = grp.reshape(t)
    pos = rank.reshape(t)
    for k in range(N_GROUPS):
        pos = pos + jnp.where(g == k, off[k], 0)
    starts = jnp.arange(t // rt + N_GROUPS, dtype=jnp.int32) * rt
    tile_group = jnp.minimum(jnp.sum((starts[:, None] >= off_end[None, :]).astype(jnp.int32), axis=1),
                             N_GROUPS - 1)
    tile_valid = (starts < off_end[N_GROUPS - 1]).astype(jnp.int32)
    last_tile = jnp.maximum(off_end[N_GROUPS - 1] // rt - 1, 0).reshape(1)
    tail = off_end[N_GROUPS - 1] + jnp.arange(N_GROUPS, dtype=jnp.int32) * rt
    fill_row = jnp.concatenate([jnp.maximum(off_end - rt, 0), jnp.minimum(tail, t + (N_GROUPS - 1) * rt)])
    fill_on = jnp.concatenate([padded > 0, tail < t + N_GROUPS * rt]).astype(jnp.int32)
    return pos, tile_group, tile_valid, last_tile, fill_row, fill_on


ROW_DMA_UNROLL = 8


def _row_copy(src_ref, src_row, dst_ref, dst_row, sem):
    return pltpu.make_async_copy(src_ref.at[pl.ds(src_row, 1)], dst_ref.at[pl.ds(dst_row, 1)], sem)


def _dispatch_kernel(pos_ref, fill_row_ref, fill_on_ref, xe_ref, xs_ref, zero_ref, sem, *, tm, rt):
    base = pl.program_id(0) * tm

    @pl.when(pl.program_id(0) == 0)
    def _():
        zero_ref[...] = jnp.zeros(zero_ref.shape, F32)
        for k in range(2 * N_GROUPS):
            @pl.when(fill_on_ref[k] == 1)
            def _():
                row = pl.multiple_of(fill_row_ref[k], rt)
                fill = pltpu.make_async_copy(zero_ref, xs_ref.at[pl.ds(row, rt)], sem)
                fill.start()
                fill.wait()

    def issue(r, c):
        _row_copy(xe_ref, r, xs_ref, pos_ref[base + r], sem).start()
        return c

    lax.fori_loop(0, tm, issue, 0, unroll=ROW_DMA_UNROLL)
    pltpu.make_async_copy(xe_ref, xs_ref.at[pl.ds(0, tm)], sem).wait()


def moe_dispatch(pos, fill_row, fill_on, xe, t_pad, rt, tm=1024):
    t, w = xe.shape
    tm = min(tm, t)
    kern = functools.partial(_dispatch_kernel, tm=tm, rt=rt)
    return pl.pallas_call(
        kern,
        grid_spec=pltpu.PrefetchScalarGridSpec(
            num_scalar_prefetch=3,
            grid=(t // tm,),
            in_specs=[pl.BlockSpec((tm, w), lambda i, *_: (i, 0))],
            out_specs=pl.BlockSpec(memory_space=pl.ANY),
            scratch_shapes=[pltpu.VMEM((rt, w), F32), pltpu.SemaphoreType.DMA],
        ),
        out_shape=jax.ShapeDtypeStruct((t_pad, w), F32),
        compiler_params=pltpu.CompilerParams(
            dimension_semantics=("arbitrary",), vmem_limit_bytes=VMEM_LIMIT),
        name="moe_dispatch",
    )(pos, fill_row, fill_on, xe)


def _moe_ffn_kernel(tg_ref, tv_ref, lt_ref, xs_ref, w1_ref, w3_ref, w2_ref, ys_ref):
    del lt_ref
    i = pl.program_id(0)
    valid = tv_ref[i] == 1

    @pl.when(valid)
    def _():
        x = xs_ref[:, :D_MODEL].astype(BF16)
        ext = xs_ref[:, D_MODEL:]
        lane = lax.broadcasted_iota(jnp.int32, ext.shape, 1)
        first = tg_ref[i] * EXPERTS_PER_GROUP

        def up(e):
            return _dot(x, w1_ref[e]), _dot(x, w3_ref[e])

        def down(e, ab):
            gcol = jnp.sum(jnp.where(lane == first + e, ext, 0.0), axis=-1, keepdims=True)
            hh = ((jax.nn.silu(ab[0]) * ab[1]) * gcol).astype(BF16)
            return _dot(hh, w2_ref[e])

        ab = up(0)
        out = None
        for e in range(EXPERTS_PER_GROUP):
            nxt = up(e + 1) if e + 1 < EXPERTS_PER_GROUP else None
            o = down(e, ab)
            out = o if out is None else out + o
            ab = nxt
        ys_ref[...] = out

    @pl.when(jnp.logical_not(valid))
    def _():
        ys_ref[...] = jnp.zeros(ys_ref.shape, F32)


def moe_ffn(tile_group, tile_valid, last_tile, xs, w1, w3, w2, layer, rt):
    t_pad, w = xs.shape
    d = D_MODEL
    widx = lambda i, tg, tv, lt: (layer, tg[i], 0, 0)
    return pl.pallas_call(
        _moe_ffn_kernel,
        grid_spec=pltpu.PrefetchScalarGridSpec(
            num_scalar_prefetch=3,
            grid=(t_pad // rt,),
            in_specs=[pl.BlockSpec((rt, w), lambda i, tg, tv, lt: (jnp.minimum(i, lt[0]), 0)),
                      pl.BlockSpec((None, EXPERTS_PER_GROUP, d, D_FF_EXPERT), widx),
                      pl.BlockSpec((None, EXPERTS_PER_GROUP, d, D_FF_EXPERT), widx),
                      pl.BlockSpec((None, EXPERTS_PER_GROUP, D_FF_EXPERT, d), widx)],
            out_specs=pl.BlockSpec((rt, d), lambda i, tg, tv, lt: (i, 0)),
        ),
        out_shape=jax.ShapeDtypeStruct((t_pad, d), F32),
        compiler_params=pltpu.CompilerParams(
            dimension_semantics=("arbitrary",), vmem_limit_bytes=VMEM_LIMIT),
        name="moe_ffn",
    )(tile_group, tile_valid, last_tile, xs, w1, w3, w2)


def _combine_kernel(pos_ref, h_ref, ys_ref, fg_ref, o_ref, ybuf_ref, sem, *, tm, final_norm):
    base = pl.program_id(0) * tm

    def issue(r, c):
        _row_copy(ys_ref, pos_ref[base + r], ybuf_ref, r, sem).start()
        return c

    lax.fori_loop(0, tm, issue, 0, unroll=ROW_DMA_UNROLL)
    pltpu.make_async_copy(ys_ref.at[pl.ds(0, tm)], ybuf_ref, sem).wait()
    y = h_ref[...] + ybuf_ref[...]
    if final_norm:
        y = _rms(y, fg_ref[...])
    o_ref[...] = y


def moe_combine(pos, h, ys, final_gain, final_norm, tm=1024):
    t, d = h.shape
    tm = min(tm, t)
    kern = functools.partial(_combine_kernel, tm=tm, final_norm=final_norm)
    return pl.pallas_call(
        kern,
        grid_spec=pltpu.PrefetchScalarGridSpec(
            num_scalar_prefetch=1,
            grid=(t // tm,),
            in_specs=[pl.BlockSpec((tm, d), lambda i, pos: (i, 0)),
                      pl.BlockSpec(memory_space=pl.ANY),
                      pl.BlockSpec((1, d), lambda i, pos: (0, 0))],
            out_specs=pl.BlockSpec((tm, d), lambda i, pos: (i, 0)),
            scratch_shapes=[pltpu.VMEM((tm, d), F32), pltpu.SemaphoreType.DMA],
        ),
        out_shape=jax.ShapeDtypeStruct((t, d), F32),
        compiler_params=pltpu.CompilerParams(
            dimension_semantics=("arbitrary",), vmem_limit_bytes=VMEM_LIMIT),
        name="moe_combine",
    )(pos, h, ys, final_gain.reshape(1, d))


def grouped_moe(h, xe, grp, rank, cnt, w1, w3, w2, layer, final_gain, final_norm, rt=512):
    t = h.shape[0]
    rt = min(rt, t)
    pos, tile_group, tile_valid, last_tile, fill_row, fill_on = _route_tables(cnt, grp, rank, t, rt)
    xs = moe_dispatch(pos, fill_row, fill_on, xe, t + N_GROUPS * rt, rt)
    ys = moe_ffn(tile_group, tile_valid, last_tile, xs, w1, w3, w2, layer, rt)
    return moe_combine(pos, h, ys, final_gain, final_norm)


def kernel(x, norm_mix_gain, w_in, gmlp_w_s, gmlp_b_s, gmlp_v_gain, diff_lambda, diff_subln_gain,
           w_up_a, w_up_b, w_up_c, w_out, norm_ffn_gain, router_w, router_bias,
           moe_w1, moe_w3, moe_w2, final_gain):
    bsz, seq, d = x.shape
    depth = w_in.shape[0]
    x2 = x.reshape(bsz * seq, d)
    router_wt = router_w.T
    w_in, w_up_a, w_up_b, w_up_c, w_out, moe_w1, moe_w3, moe_w2 = (
        w.astype(BF16) for w in (w_in, w_up_a, w_up_b, w_up_c, w_out, moe_w1, moe_w3, moe_w2))
    for l in range(depth):
        lambda_init = 0.8 - 0.6 * math.exp(-0.3 * l)
        proj = in_proj(x2, norm_mix_gain[l], w_in, l)
        yb = diff_attn(proj, diff_lambda[l], diff_subln_gain[l], bsz, seq, lambda_init)
        yc = stick_attn(proj, bsz, seq)
        h, xe, grp, rank, cnt = merge(x2, proj, yb, yc, gmlp_w_s[l], gmlp_b_s[l].T, gmlp_v_gain[l],
                                      w_up_a, w_up_b, w_up_c, w_out, l, norm_ffn_gain[l], router_wt, router_bias)
        x2 = grouped_moe(h, xe, grp, rank, cnt, moe_w1, moe_w3, moe_w2, l, final_gain,
                         final_norm=(l == depth - 1))
    return x2.reshape(bsz, seq, d)
```

```python
import functools
import math

import jax
import jax.numpy as jnp
from jax import lax
from jax.experimental import pallas as pl
from jax.experimental.pallas import tpu as pltpu

D_MODEL = 1024
BLOCK = 128
A_GROUPS = 4
A_WIDTH = 512
B_HEADS = 4
B_QK_DIM = 64
B_V_DIM = 128
C_HEADS = 8
C_HEAD_DIM = 64
C_WIDTH = 512
N_EXPERTS = 16
N_GROUPS = 4
EXPERTS_PER_GROUP = 4
D_FF_EXPERT = 512
IN_WIDTH = 7168
XE_WIDTH = D_MODEL + 128
EPS = 1e-6

LANES = 128
VMEM_LIMIT = 56 * 1024 * 1024
F32 = jnp.float32
BF16 = jnp.bfloat16
NEG_BIG = -1e30
EXP_UNDERFLOW = -104.0
EXP2_UNDERFLOW = -152.0

LOG2E = 1.4426950408889634

COL_QB, COL_KB, COL_VB = 8, 12, 16
QB_COLS = (2 * A_WIDTH, 2 * A_WIDTH + 512)
QC_COLS = (2 * A_WIDTH + 3 * 512, 2 * A_WIDTH + 4 * 512)


def _nt_dot(a, b):
    return lax.dot_general(a, b, (((1,), (1,)), ((), ())), preferred_element_type=F32)


def _dot(a, b):
    return jnp.dot(a, b, preferred_element_type=F32)


def _sigmoid(x):
    return 0.5 * jnp.tanh(0.5 * x) + 0.5


def _rms(xf, gain):
    return xf * lax.rsqrt(jnp.mean(xf * xf, axis=-1, keepdims=True) + EPS) * gain


def _in_proj_kernel(x_ref, g_ref, w_ref, cs_ref, o_ref, xn_ref):
    @pl.when(pl.program_id(1) == 0)
    def _():
        xn_ref[...] = _rms(x_ref[...], g_ref[...]).astype(BF16)

    o_ref[...] = (_dot(xn_ref[...], w_ref[...]) * cs_ref[...]).astype(BF16)


def _query_scale_row(n):
    col = jnp.arange(n)
    s = jnp.ones((n,), F32)
    s = jnp.where((col >= QB_COLS[0]) & (col < QB_COLS[1]), LOG2E * B_QK_DIM ** -0.5, s)
    s = jnp.where((col >= QC_COLS[0]) & (col < QC_COLS[1]), C_HEAD_DIM ** -0.5, s)
    return s.reshape(1, n)


def in_proj(x2, gain, w_all, layer, tm=2048, tn=1792):
    t, d = x2.shape
    n = w_all.shape[2]
    tm = min(tm, t)
    return pl.pallas_call(
        _in_proj_kernel,
        grid=(t // tm, n // tn),
        in_specs=[
            pl.BlockSpec((tm, d), lambda i, j: (i, 0)),
            pl.BlockSpec((1, d), lambda i, j: (0, 0)),
            pl.BlockSpec((None, d, tn), lambda i, j: (layer, 0, j)),
            pl.BlockSpec((1, tn), lambda i, j: (0, j)),
        ],
        out_specs=pl.BlockSpec((tm, tn), lambda i, j: (i, j)),
        out_shape=jax.ShapeDtypeStruct((t, n), BF16),
        scratch_shapes=[pltpu.VMEM((tm, d), BF16)],
        compiler_params=pltpu.CompilerParams(
            dimension_semantics=("arbitrary", "arbitrary"), vmem_limit_bytes=VMEM_LIMIT),
        name="in_proj",
    )(x2, gain.reshape(1, d), w_all, _query_scale_row(n))


def _diff_attn_kernel(lam_ref, sg_ref, q_ref, k_ref, v_ref, o_ref, m_ref, l_ref, acc_ref, s_ref, p_ref, kmax_ref,
                      j0_ref,
                      *, tq, hps, nq, lambda_init):
    tk = tq
    hp = pl.program_id(1)
    qi = pl.program_id(2)
    nc = tk // LANES
    units = [(hh, mi) for hh in range(hps) for mi in range(2)]
    head_cols = [slice(hh * LANES, (hh + 1) * LANES) for hh in range(hps)]

    def head_slope(h):
        sl = jnp.where(h == 0, 0.25, jnp.where(h == 1, 0.0625, jnp.where(h == 2, 0.015625, 0.00390625)))
        return sl.astype(F32) * LOG2E

    slopes = [head_slope(hp * hps + hh) for hh in range(hps)]

    lane = lax.broadcasted_iota(jnp.int32, (tq, LANES), 1)
    qs = []
    for hh, mi in units:
        q = q_ref[:, head_cols[hh]]
        keep = (lane < B_QK_DIM) if mi == 0 else (lane >= B_QK_DIM)
        qs.append(jnp.where(keep, q, jnp.zeros_like(q)))

    m_ref[...] = jnp.full(m_ref.shape, NEG_BIG, F32)
    l_ref[...] = jnp.zeros(l_ref.shape, F32)
    acc_ref[...] = jnp.zeros(acc_ref.shape, F32)

    kcol = lax.broadcasted_iota(jnp.int32, (1, LANES), 1)
    row_minus_col = lax.broadcasted_iota(jnp.int32, (tq, LANES), 0) - lane
    q0 = qi * tq

    def load(ref, kb, hh):
        return ref[pl.ds(pl.multiple_of(kb * tk, tk), tk), head_cols[hh]]

    def scores(kb):
        kks = [load(k_ref, kb, hh) for hh in range(hps)]
        return [_nt_dot(qs[u], kks[hh]) for u, (hh, _) in enumerate(units)]

    def softmax_step(u, s, kb, masked):
        off = kb * tk - q0
        slope = slopes[units[u][0]]
        cols = []
        for c in range(nc):
            sc = s[:, c * LANES:(c + 1) * LANES] + slope * (kcol + (off + c * LANES)).astype(F32)
            if masked:
                sc = jnp.where(row_minus_col >= (off + c * LANES), sc, NEG_BIG)
            cols.append(sc)
        mx = cols[0]
        for sc in cols[1:]:
            mx = jnp.maximum(mx, sc)
        m_prev = m_ref[u]
        m_new = jnp.maximum(m_prev, jnp.max(mx, axis=-1, keepdims=True))
        alpha = jnp.exp2(m_prev - m_new)
        ps = [jnp.exp2(sc - m_new) for sc in cols]
        psum = ps[0]
        for pc in ps[1:]:
            psum = psum + pc
        l_ref[u] = alpha * l_ref[u] + psum
        acc_ref[u] = alpha * acc_ref[u]
        m_ref[u] = m_new
        return jnp.concatenate([pc.astype(BF16) for pc in ps], axis=1)

    lanes1 = lax.broadcasted_iota(jnp.int32, (1, LANES), 1)

    hs_i = lax.broadcasted_iota(jnp.int32, (2 * LANES, LANES), 0) % LANES
    hs_j = lax.broadcasted_iota(jnp.int32, (2 * LANES, LANES), 1)
    half_sum = jnp.where((hs_i < B_QK_DIM) == (hs_j < B_QK_DIM), 1.0, 0.0).astype(BF16)

    def half_sums(x):
        hi = x.astype(BF16)
        lo = (x - hi.astype(F32)).astype(BF16)
        return _dot(jnp.concatenate([hi, lo], axis=1), half_sum)

    steepest = slopes[0]
    for sl in slopes[1:]:
        steepest = jnp.maximum(steepest, sl)
    far_enough = lambda first_row: steepest * (first_row - tk + 1).astype(F32) > -EXP2_UNDERFLOW

    @pl.when(jnp.logical_and(qi == 0, far_enough(jnp.int32((nq - 1) * tq))))
    def _():
        for hh in range(hps):
            rows = [jnp.zeros((1, LANES), F32) for _ in range(2)]
            for kb in range(nq):
                kk = k_ref[kb * tk:(kb + 1) * tk, head_cols[hh]].astype(F32)
                n2 = jnp.sqrt(jnp.max(half_sums(kk * kk), axis=0, keepdims=True))
                for mi in range(2):
                    at = mi * B_QK_DIM
                    rows[mi] = jnp.where(lanes1 == kb, n2[:, at:at + 1], rows[mi])
            for mi in range(2):
                kmax_ref[2 * hh + mi, 0:1, :] = rows[mi]

    j0_ref[0] = 0

    @pl.when(far_enough(q0))
    def _():
        worst = None
        for hh in range(hps):
            qf = q_ref[:, head_cols[hh]].astype(F32)
            k_own = k_ref[pl.ds(pl.multiple_of(q0, tq), tq), head_cols[hh]].astype(F32)
            score_min = jnp.min(half_sums(qf * k_own), axis=0, keepdims=True)
            norm_max = jnp.sqrt(jnp.max(half_sums(qf * qf), axis=0, keepdims=True))
            top_bias = slopes[hh] * ((lanes1 + 1) * tk - 1 - q0).astype(F32)
            for mi in range(2):
                u = 2 * hh + mi
                at = mi * B_QK_DIM
                gap = norm_max[:, at:at + 1] * kmax_ref[u, 0:1, :] + top_bias - score_min[:, at:at + 1]
                worst = gap if worst is None else jnp.maximum(worst, gap)
        needed = jnp.logical_or(worst >= EXP2_UNDERFLOW, lanes1 >= qi)
        j0_ref[0] = jnp.min(jnp.where(needed, lanes1, LANES))

    j0 = j0_ref[0]

    odd = lax.rem(qi - j0, 2)
    s0 = scores(j0)
    for u in range(len(units)):
        s_ref[odd, u] = s0[u]
        p_ref[odd, u] = jnp.zeros((tq, tk), BF16)

    def pending_pv(j, slot):
        vs = [load(v_ref, jnp.maximum(j - 1, 0), hh) for hh in range(hps)]
        return [_dot(p_ref[slot, u], vs[hh]) for u, (hh, _) in enumerate(units)]

    def step(j, slot):
        nxt = 1 - slot
        pv = pending_pv(j, slot)
        sn = scores(j + 1)
        for u in range(len(units)):
            acc_ref[u] += pv[u]
            s_ref[nxt, u] = sn[u]
        for u in range(len(units)):
            p_ref[nxt, u] = softmax_step(u, s_ref[slot, u], j, False)

    @pl.when(odd == 1)
    def _():
        step(j0, 1)

    def body(t, c):
        j = j0 + odd + 2 * t
        step(j, 0)
        step(j + 1, 1)
        return c

    lax.fori_loop(0, (qi - j0) // 2, body, 0)
    pv = pending_pv(qi, 0)
    for u in range(len(units)):
        acc_ref[u] += pv[u]
    vdiag = [load(v_ref, qi, hh) for hh in range(hps)]
    for u, (hh, _) in enumerate(units):
        p = softmax_step(u, s_ref[0, u], qi, True)
        acc_ref[u] += _dot(p, vdiag[hh])

    lp = lam_ref[...]
    s1 = jnp.sum(lp[0:1] * lp[1:2], axis=-1, keepdims=True)
    s2 = jnp.sum(lp[2:3] * lp[3:4], axis=-1, keepdims=True)
    lam = jnp.exp(s1) - jnp.exp(s2) + lambda_init
    for hh in range(hps):
        l0 = jnp.sum(l_ref[2 * hh], axis=-1, keepdims=True)
        l1 = jnp.sum(l_ref[2 * hh + 1], axis=-1, keepdims=True)
        o = acc_ref[2 * hh] / l0 - lam * (acc_ref[2 * hh + 1] / l1)
        o_ref[:, head_cols[hh]] = (_rms(o, sg_ref[...]) * (1.0 - lambda_init)).astype(BF16)


def diff_attn(proj, lam_params, subln_g, bsz, seq, lambda_init, tq=512, hps=1):
    tq = min(tq, seq)
    nq = seq // tq
    n_units = 2 * hps
    w = hps * LANES
    kern = functools.partial(_diff_attn_kernel, tq=tq, hps=hps, nq=nq, lambda_init=lambda_init)
    return pl.pallas_call(
        kern,
        grid=(bsz, B_HEADS // hps, nq),
        in_specs=[
            pl.BlockSpec((4, B_QK_DIM), lambda b, h, i: (0, 0)),
            pl.BlockSpec((1, B_V_DIM), lambda b, h, i: (0, 0)),
            pl.BlockSpec((tq, w), lambda b, h, i: (b * nq + i, COL_QB // hps + h)),
            pl.BlockSpec((seq, w), lambda b, h, i: (b, COL_KB // hps + h)),
            pl.BlockSpec((seq, w), lambda b, h, i: (b, COL_VB // hps + h)),
        ],
        out_specs=pl.BlockSpec((tq, w), lambda b, h, i: (b * nq + i, h)),
        out_shape=jax.ShapeDtypeStruct((bsz * seq, B_HEADS * B_V_DIM), BF16),
        scratch_shapes=[pltpu.VMEM((n_units, tq, LANES), F32), pltpu.VMEM((n_units, tq, LANES), F32),
                        pltpu.VMEM((n_units, tq, B_V_DIM), F32),
                        pltpu.VMEM((2, n_units, tq, tq), F32), pltpu.VMEM((2, n_units, tq, tq), BF16),
                        pltpu.VMEM((n_units, 8, LANES), F32), pltpu.SMEM((1,), jnp.int32)],
        compiler_params=pltpu.CompilerParams(
            dimension_semantics=("arbitrary", "arbitrary", "arbitrary"), vmem_limit_bytes=VMEM_LIMIT),
        name="diff_attn",
    )(lam_params, subln_g.reshape(1, B_V_DIM), proj, proj, proj)


def _stick_kernel(q_ref, k_ref, v_ref, o_ref, qs_ref, carry_ref, acc_ref, *, tq, n_chains):
    tk = tq
    n_pairs = C_HEADS // 2
    gi = pl.program_id(1)
    lane = lax.broadcasted_iota(jnp.int32, (tq, LANES), 1)
    low = lane < C_HEAD_DIM
    units = [(c, p) for c in range(n_chains) for p in range(n_pairs)]
    for c, p in units:
        q = q_ref[c * tq:(c + 1) * tq, p * LANES:(p + 1) * LANES]
        zero = jnp.zeros_like(q)
        qs_ref[c, p, 0:tq] = jnp.where(low, q, zero)
        qs_ref[c, p, tq:2 * tq] = jnp.where(low, zero, q)
    carry_ref[...] = jnp.zeros(carry_ref.shape, F32)
    acc_ref[...] = jnp.zeros(acc_ref.shape, F32)

    row = lax.broadcasted_iota(jnp.int32, (2 * tq, tk), 0)
    col = lax.broadcasted_iota(jnp.int32, (2 * tq, tk), 1)
    strict = (row % tq) > col
    uj = lax.broadcasted_iota(jnp.int32, (2 * tk, tk + LANES), 0) % tk
    us = lax.broadcasted_iota(jnp.int32, (2 * tk, tk + LANES), 1)
    w2 = jnp.where(jnp.logical_or(uj > us, us >= tk), 1.0, 0.0).astype(BF16)
    vlow = lax.broadcasted_iota(jnp.int32, (tk, LANES), 1) < C_HEAD_DIM

    def step(n, masked):
        kbs = [gi * n_chains + c - n for c in range(n_chains)]
        starts = [pl.multiple_of(jnp.maximum(kb, 0) * tk, tk) for kb in kbs]
        gone = [jnp.where(kb < 0, NEG_BIG, 0.0).astype(F32) for kb in kbs]
        cols = [slice(p * LANES, (p + 1) * LANES) for p in range(n_pairs)]
        zs = [_nt_dot(qs_ref[c, p], k_ref[pl.ds(starts[c], tk), cols[p]]) for c, p in units]
        lbs, rs = [], []
        for z in zs:
            sp = jnp.log(1.0 + jnp.exp(-jnp.abs(z)))
            lb = jnp.minimum(z, 0.0) - sp
            lk = lb - z
            if masked:
                lk = jnp.where(strict, lk, 0.0)
            hi = lk.astype(BF16)
            lo = (lk - hi.astype(F32)).astype(BF16)
            lbs.append(lb)
            rs.append(_dot(jnp.concatenate([hi, lo], axis=1), w2))
        alive = None
        for u, (c, p) in enumerate(units):
            carry = carry_ref[c, p]
            a = jnp.exp(lbs[u] + rs[u][:, :tk] + (carry + gone[c]))
            if masked:
                a = jnp.where(strict, a, 0.0)
            carry = carry + rs[u][:, tk:]
            carry_ref[c, p] = carry
            left = carry + jnp.where(kbs[c] < 1, NEG_BIG, 0.0).astype(F32)
            alive = left if alive is None else jnp.maximum(alive, left)
            ab = a.astype(BF16)
            vv = v_ref[pl.ds(starts[c], tk), cols[p]]
            zero = jnp.zeros_like(vv)
            vst = jnp.concatenate([jnp.where(vlow, vv, zero), jnp.where(vlow, zero, vv)], axis=0)
            acc_ref[c, p] += _dot(jnp.concatenate([ab[:tq], ab[tq:]], axis=1), vst)
        return jnp.max(alive)

    def cond(st):
        _, alive = st
        return alive > EXP_UNDERFLOW

    def body(st):
        n, _ = st
        return n + 1, step(n, False)

    lax.while_loop(cond, body, (1, step(0, True)))
    for c, p in units:
        o_ref[c * tq:(c + 1) * tq, p * LANES:(p + 1) * LANES] = acc_ref[c, p].astype(BF16)


def stick_attn(proj, bsz, seq, tq=128, n_chains=4):
    n_chains = min(n_chains, seq // tq)
    rows = tq * n_chains
    nq = seq // rows
    kern = functools.partial(_stick_kernel, tq=tq, n_chains=n_chains)
    n_pairs = C_HEADS // 2
    return pl.pallas_call(
        kern,
        grid=(bsz, nq),
        in_specs=[
            pl.BlockSpec((rows, C_WIDTH), lambda b, i: (b * nq + i, 5)),
            pl.BlockSpec((seq, C_WIDTH), lambda b, i: (b, 6)),
            pl.BlockSpec((seq, C_WIDTH), lambda b, i: (b, 7)),
        ],
        out_specs=pl.BlockSpec((rows, C_WIDTH), lambda b, i: (b * nq + i, 0)),
        out_shape=jax.ShapeDtypeStruct((bsz * seq, C_WIDTH), BF16),
        scratch_shapes=[pltpu.VMEM((n_chains, n_pairs, 2 * tq, LANES), BF16),
                        pltpu.VMEM((n_chains, n_pairs, 2 * tq, LANES), F32),
                        pltpu.VMEM((n_chains, n_pairs, tq, LANES), F32)],
        compiler_params=pltpu.CompilerParams(
            dimension_semantics=("arbitrary", "arbitrary"), vmem_limit_bytes=VMEM_LIMIT),
        name="stick_attn",
    )(proj, proj, proj)


def _top2_route(sel, aff):
    def top2_sum(a, b, c, d):
        hi1, lo1 = jnp.maximum(a, b), jnp.minimum(a, b)
        hi2, lo2 = jnp.maximum(c, d), jnp.minimum(c, d)
        return jnp.maximum(hi1, hi2) + jnp.maximum(jnp.minimum(hi1, hi2), jnp.maximum(lo1, lo2))

    scores = [top2_sum(*sel[4 * g:4 * g + 4]) for g in range(N_GROUPS)]
    best = jnp.zeros_like(scores[0], dtype=jnp.int32)
    best_score = scores[0]
    for g in range(1, N_GROUPS):
        better = scores[g] > best_score
        best = jnp.where(better, g, best)
        best_score = jnp.where(better, scores[g], best_score)

    def pick(rows, i):
        out = rows[i]
        for g in range(1, N_GROUPS):
            out = jnp.where(best == g, rows[4 * g + i], out)
        return out

    cs = [pick(sel, i) for i in range(EXPERTS_PER_GROUP)]
    ca = [pick(aff, i) for i in range(EXPERTS_PER_GROUP)]

    def argmax_first(vals, exclude=None):
        idx = jnp.full(vals[0].shape, -1, jnp.int32)
        cur = jnp.full(vals[0].shape, -jnp.inf, F32)
        for i, v in enumerate(vals):
            ok = v > cur
            if exclude is not None:
                ok = jnp.logical_and(ok, exclude != i)
            idx = jnp.where(ok, i, idx)
            cur = jnp.where(ok, v, cur)
        return idx

    i1 = argmax_first(cs)
    i2 = argmax_first(cs, exclude=i1)

    def take(vals, idx):
        out = vals[0]
        for i in range(1, len(vals)):
            out = jnp.where(idx == i, vals[i], out)
        return out

    w1, w2 = take(ca, i1), take(ca, i2)
    tot = w1 + w2
    w1, w2 = w1 / tot, w2 / tot
    e1 = best * EXPERTS_PER_GROUP + i1
    e2 = best * EXPERTS_PER_GROUP + i2
    gates = [jnp.where(e1 == e, w1, 0.0) + jnp.where(e2 == e, w2, 0.0) for e in range(N_EXPERTS)]
    return gates, best


def _merge_kernel(x_ref, u_ref, v_ref, g0_ref, g1_ref, g2_ref, yb_ref, yc_ref,
                  ws_ref, bst_ref, gv_ref, wa_ref, wb_ref, wc_ref, wo_ref, ng_ref, rwt_ref, rb_ref,
                  h_ref, xe_ref, grp_ref, rank_ref, cnt_ref, ya_ref, run_ref, *, tm):
    u = jax.nn.gelu(u_ref[...].astype(F32))
    v = _rms(jax.nn.gelu(v_ref[...].astype(F32)), gv_ref[...]).astype(BF16)

    r = lax.broadcasted_iota(jnp.int32, (BLOCK, BLOCK), 0)
    c = lax.broadcasted_iota(jnp.int32, (BLOCK, BLOCK), 1)
    causal = r >= c
    for g in range(A_GROUPS):
        w = jnp.where(causal, ws_ref[g], 0.0).astype(BF16)
        bias = bst_ref[:, g:g + 1]
        cols = slice(g * LANES, (g + 1) * LANES)
        for ch in range(tm // BLOCK):
            rows = slice(ch * BLOCK, (ch + 1) * BLOCK)
            mixed = _dot(w, v[rows, cols]) + bias
            ya_ref[rows, cols] = (u[rows, cols] * mixed).astype(BF16)

    merged = _sigmoid(g0_ref[...].astype(F32)) * _dot(ya_ref[...], wa_ref[...])
    merged += _sigmoid(g1_ref[...].astype(F32)) * _dot(yb_ref[...], wb_ref[...])
    merged += _sigmoid(g2_ref[...].astype(F32)) * _dot(yc_ref[...], wc_ref[...])
    h = x_ref[...] + _dot(merged.astype(BF16), wo_ref[...])
    h_ref[...] = h

    xn = _rms(h, ng_ref[...])
    xe_ref[:, :D_MODEL] = xn

    xh = xn.astype(BF16)
    xl = (xn - xh.astype(F32)).astype(BF16)
    rw = rwt_ref[...]
    rh = rw.astype(BF16)
    rl = (rw - rh.astype(F32)).astype(BF16)
    logits = _nt_dot(rh, xh) + (_nt_dot(rh, xl) + _nt_dot(rl, xh))
    aff = jax.nn.sigmoid(logits)
    sel = aff + rb_ref[...]
    gates, best = _top2_route([sel[e:e + 1] for e in range(N_EXPERTS)],
                              [aff[e:e + 1] for e in range(N_EXPERTS)])
    gate_rows = jnp.concatenate(gates + [jnp.zeros((LANES - N_EXPERTS, tm), F32)], axis=0)
    xe_ref[:, D_MODEL:] = gate_rows.T

    @pl.when(pl.program_id(0) == 0)
    def _():
        run_ref[...] = jnp.zeros(run_ref.shape, F32)

    gid = lax.broadcasted_iota(jnp.int32, (8, tm), 0)
    member = (gid == best).astype(F32)
    before = lax.broadcasted_iota(jnp.int32, (tm, tm), 0) < lax.broadcasted_iota(jnp.int32, (tm, tm), 1)
    prefix = _dot(member.astype(BF16), before.astype(BF16))
    run = run_ref[...]
    rank = jnp.sum(member * (prefix + run[:, 0:1]), axis=0, keepdims=True)
    rank_ref[...] = rank.astype(jnp.int32)
    grp_ref[...] = best
    run = run + jnp.sum(member, axis=-1, keepdims=True)
    run_ref[...] = run
    cnt_ref[...] = run


def merge(x2, proj, yb, yc, w_s, b_s_t, g_v, wa, wb, wc, wo, layer, norm_g, router_wt, router_b, tm=512):
    t, d = x2.shape
    tm = min(tm, t)
    row = lambda i: (i, 0)
    const2 = lambda i: (0, 0)
    of_layer = lambda i: (layer, 0, 0)
    kern = functools.partial(_merge_kernel, tm=tm)
    return pl.pallas_call(
        kern,
        grid=(t // tm,),
        in_specs=[
            pl.BlockSpec((tm, d), row),
            pl.BlockSpec((tm, A_WIDTH), lambda i: (i, 0)),
            pl.BlockSpec((tm, A_WIDTH), lambda i: (i, 1)),
            pl.BlockSpec((tm, d), lambda i: (i, 4)),
            pl.BlockSpec((tm, d), lambda i: (i, 5)),
            pl.BlockSpec((tm, d), lambda i: (i, 6)),
            pl.BlockSpec((tm, A_WIDTH), row),
            pl.BlockSpec((tm, C_WIDTH), row),
            pl.BlockSpec((A_GROUPS, BLOCK, BLOCK), lambda i: (0, 0, 0)),
            pl.BlockSpec((BLOCK, A_GROUPS), const2),
            pl.BlockSpec((1, A_WIDTH), const2),
            pl.BlockSpec((None, A_WIDTH, d), of_layer),
            pl.BlockSpec((None, A_WIDTH, d), of_layer),
            pl.BlockSpec((None, C_WIDTH, d), of_layer),
            pl.BlockSpec((None, d, d), of_layer),
            pl.BlockSpec((1, d), const2),
            pl.BlockSpec((N_EXPERTS, d), const2),
            pl.BlockSpec((N_EXPERTS, 1), const2),
        ],
        out_specs=[
            pl.BlockSpec((tm, d), row),
            pl.BlockSpec((tm, XE_WIDTH), row),
            pl.BlockSpec((1, tm), lambda i: (0, i)),
            pl.BlockSpec((1, tm), lambda i: (0, i)),
            pl.BlockSpec((8, LANES), const2),
        ],
        out_shape=[
            jax.ShapeDtypeStruct((t, d), F32),
            jax.ShapeDtypeStruct((t, XE_WIDTH), F32),
            jax.ShapeDtypeStruct((1, t), jnp.int32),
            jax.ShapeDtypeStruct((1, t), jnp.int32),
            jax.ShapeDtypeStruct((8, LANES), F32),
        ],
        scratch_shapes=[pltpu.VMEM((tm, A_WIDTH), BF16), pltpu.VMEM((8, LANES), F32)],
        compiler_params=pltpu.CompilerParams(
            dimension_semantics=("arbitrary",), vmem_limit_bytes=VMEM_LIMIT),
        name="merge",
    )(x2, proj, proj, proj, proj, proj, yb, yc, w_s, b_s_t, g_v.reshape(1, A_WIDTH),
      wa, wb, wc, wo, norm_g.reshape(1, d), router_wt, router_b.reshape(N_EXPERTS, 1))


def _route_tables(cnt, grp, rank, t, rt):
    cnt = cnt[:N_GROUPS, 0].astype(jnp.int32)
    padded = (cnt + rt - 1) // rt * rt
    off_end = jnp.cumsum(padded)
    off = off_end - padded
    g = grp.reshape(t)
    pos = rank.reshape(t)
    for k in range(N_GROUPS):
        pos = pos + jnp.where(g == k, off[k], 0)
    starts = jnp.arange(t // rt + N_GROUPS, dtype=jnp.int32) * rt
    tile_group = jnp.minimum(jnp.sum((starts[:, None] >= off_end[None, :]).astype(jnp.int32), axis=1),
                             N_GROUPS - 1)
    tile_valid = (starts < off_end[N_GROUPS - 1]).astype(jnp.int32)
    last_tile = jnp.maximum(off_end[N_GROUPS - 1] // rt - 1, 0).reshape(1)
    tail = off_end[N_GROUPS - 1] + jnp.arange(N_GROUPS, dtype=jnp.int32) * rt
    fill_row = jnp.concatenate([jnp.maximum(off_end - rt, 0), jnp.minimum(tail, t + (N_GROUPS - 1) * rt)])
    fill_on = jnp.concatenate([padded > 0, tail < t + N_GROUPS * rt]).astype(jnp.int32)
    return pos, tile_group, tile_valid, last_tile, fill_row, fill_on


ROW_DMA_UNROLL = 16


def _row_copy(src_ref, src_row, dst_ref, dst_row, sem):
    return pltpu.make_async_copy(src_ref.at[pl.ds(src_row, 1)], dst_ref.at[pl.ds(dst_row, 1)], sem)


def _dispatch_kernel(pos_ref, fill_row_ref, fill_on_ref, xe_ref, xs_ref, zero_ref, sem, *, tm, rt):
    base = pl.program_id(0) * tm

    @pl.when(pl.program_id(0) == 0)
    def _():
        zero_ref[...] = jnp.zeros(zero_ref.shape, F32)
        for k in range(2 * N_GROUPS):
            @pl.when(fill_on_ref[k] == 1)
            def _():
                row = pl.multiple_of(fill_row_ref[k], rt)
                fill = pltpu.make_async_copy(zero_ref, xs_ref.at[pl.ds(row, rt)], sem)
                fill.start()
                fill.wait()

    def issue(r, c):
        _row_copy(xe_ref, r, xs_ref, pos_ref[base + r], sem).start()
        return c

    lax.fori_loop(0, tm, issue, 0, unroll=ROW_DMA_UNROLL)
    pltpu.make_async_copy(xe_ref, xs_ref.at[pl.ds(0, tm)], sem).wait()


def moe_dispatch(pos, fill_row, fill_on, xe, t_pad, rt, tm=1024):
    t, w = xe.shape
    tm = min(tm, t)
    kern = functools.partial(_dispatch_kernel, tm=tm, rt=rt)
    return pl.pallas_call(
        kern,
        grid_spec=pltpu.PrefetchScalarGridSpec(
            num_scalar_prefetch=3,
            grid=(t // tm,),
            in_specs=[pl.BlockSpec((tm, w), lambda i, *_: (i, 0))],
            out_specs=pl.BlockSpec(memory_space=pl.ANY),
            scratch_shapes=[pltpu.VMEM((rt, w), F32), pltpu.SemaphoreType.DMA],
        ),
        out_shape=jax.ShapeDtypeStruct((t_pad, w), F32),
        compiler_params=pltpu.CompilerParams(
            dimension_semantics=("arbitrary",), vmem_limit_bytes=VMEM_LIMIT),
        name="moe_dispatch",
    )(pos, fill_row, fill_on, xe)


def _moe_ffn_kernel(tg_ref, tv_ref, lt_ref, xs_ref, w1_ref, w3_ref, w2_ref, ys_ref):
    del lt_ref
    i = pl.program_id(0)
    valid = tv_ref[i] == 1

    @pl.when(valid)
    def _():
        x = xs_ref[:, :D_MODEL].astype(BF16)
        ext = xs_ref[:, D_MODEL:]
        lane = lax.broadcasted_iota(jnp.int32, ext.shape, 1)
        first = tg_ref[i] * EXPERTS_PER_GROUP

        def up(e):
            return _dot(x, w1_ref[e]), _dot(x, w3_ref[e])

        def down(e, ab):
            gcol = jnp.sum(jnp.where(lane == first + e, ext, 0.0), axis=-1, keepdims=True)
            hh = ((jax.nn.silu(ab[0]) * ab[1]) * gcol).astype(BF16)
            return _dot(hh, w2_ref[e])

        ab = up(0)
        out = None
        for e in range(EXPERTS_PER_GROUP):
            nxt = up(e + 1) if e + 1 < EXPERTS_PER_GROUP else None
            o = down(e, ab)
            out = o if out is None else out + o
            ab = nxt
        ys_ref[...] = out

    @pl.when(jnp.logical_not(valid))
    def _():
        ys_ref[...] = jnp.zeros(ys_ref.shape, F32)


def moe_ffn(tile_group, tile_valid, last_tile, xs, w1, w3, w2, layer, rt):
    t_pad, w = xs.shape
    d = D_MODEL
    widx = lambda i, tg, tv, lt: (layer, tg[i], 0, 0)
    return pl.pallas_call(
        _moe_ffn_kernel,
        grid_spec=pltpu.PrefetchScalarGridSpec(
            num_scalar_prefetch=3,
            grid=(t_pad // rt,),
            in_specs=[pl.BlockSpec((rt, w), lambda i, tg, tv, lt: (jnp.minimum(i, lt[0]), 0)),
                      pl.BlockSpec((None, EXPERTS_PER_GROUP, d, D_FF_EXPERT), widx),
                      pl.BlockSpec((None, EXPERTS_PER_GROUP, d, D_FF_EXPERT), widx),
                      pl.BlockSpec((None, EXPERTS_PER_GROUP, D_FF_EXPERT, d), widx)],
            out_specs=pl.BlockSpec((rt, d), lambda i, tg, tv, lt: (i, 0)),
        ),
        out_shape=jax.ShapeDtypeStruct((t_pad, d), F32),
        compiler_params=pltpu.CompilerParams(
            dimension_semantics=("arbitrary",), vmem_limit_bytes=VMEM_LIMIT),
        name="moe_ffn",
    )(tile_group, tile_valid, last_tile, xs, w1, w3, w2)


def _combine_kernel(pos_ref, h_ref, ys_ref, fg_ref, o_ref, ybuf_ref, sem, *, tm, final_norm):
    base = pl.program_id(0) * tm

    def issue(r, c):
        _row_copy(ys_ref, pos_ref[base + r], ybuf_ref, r, sem).start()
        return c

    lax.fori_loop(0, tm, issue, 0, unroll=ROW_DMA_UNROLL)
    pltpu.make_async_copy(ys_ref.at[pl.ds(0, tm)], ybuf_ref, sem).wait()
    y = h_ref[...] + ybuf_ref[...]
    if final_norm:
        y = _rms(y, fg_ref[...])
    o_ref[...] = y


def moe_combine(pos, h, ys, final_gain, final_norm, tm=1024):
    t, d = h.shape
    tm = min(tm, t)
    kern = functools.partial(_combine_kernel, tm=tm, final_norm=final_norm)
    return pl.pallas_call(
        kern,
        grid_spec=pltpu.PrefetchScalarGridSpec(
            num_scalar_prefetch=1,
            grid=(t // tm,),
            in_specs=[pl.BlockSpec((tm, d), lambda i, pos: (i, 0)),
                      pl.BlockSpec(memory_space=pl.ANY),
                      pl.BlockSpec((1, d), lambda i, pos: (0, 0))],
            out_specs=pl.BlockSpec((tm, d), lambda i, pos: (i, 0)),
            scratch_shapes=[pltpu.VMEM((tm, d), F32), pltpu.SemaphoreType.DMA],
        ),
        out_shape=jax.ShapeDtypeStruct((t, d), F32),
        compiler_params=pltpu.CompilerParams(
            dimension_semantics=("arbitrary",), vmem_limit_bytes=VMEM_LIMIT),
        name="moe_combine",
    )(pos, h, ys, final_gain.reshape(1, d))


def grouped_moe(h, xe, grp, rank, cnt, w1, w3, w2, layer, final_gain, final_norm, rt=512):
    t = h.shape[0]
    rt = min(rt, t)
    pos, tile_group, tile_valid, last_tile, fill_row, fill_on = _route_tables(cnt, grp, rank, t, rt)
    xs = moe_dispatch(pos, fill_row, fill_on, xe, t + N_GROUPS * rt, rt)
    ys = moe_ffn(tile_group, tile_valid, last_tile, xs, w1, w3, w2, layer, rt)
    return moe_combine(pos, h, ys, final_gain, final_norm)


def kernel(x, norm_mix_gain, w_in, gmlp_w_s, gmlp_b_s, gmlp_v_gain, diff_lambda, diff_subln_gain,
           w_up_a, w_up_b, w_up_c, w_out, norm_ffn_gain, router_w, router_bias,
           moe_w1, moe_w3, moe_w2, final_gain):
    bsz, seq, d = x.shape
    depth = w_in.shape[0]
    x2 = x.reshape(bsz * seq, d)
    router_wt = router_w.T
    w_in, w_up_a, w_up_b, w_up_c, w_out, moe_w1, moe_w3, moe_w2 = (
        w.astype(BF16) for w in (w_in, w_up_a, w_up_b, w_up_c, w_out, moe_w1, moe_w3, moe_w2))
    for l in range(depth):
        lambda_init = 0.8 - 0.6 * math.exp(-0.3 * l)
        proj = in_proj(x2, norm_mix_gain[l], w_in, l)
        yb = diff_attn(proj, diff_lambda[l], diff_subln_gain[l], bsz, seq, lambda_init)
        yc = stick_attn(proj, bsz, seq)
        h, xe, grp, rank, cnt = merge(x2, proj, yb, yc, gmlp_w_s[l], gmlp_b_s[l].T, gmlp_v_gain[l],
                                      w_up_a, w_up_b, w_up_c, w_out, l, norm_ffn_gain[l], router_wt, router_bias)
        x2 = grouped_moe(h, xe, grp, rank, cnt, moe_w1, moe_w3, moe_w2, l, final_gain,
                         final_norm=(l == depth - 1))
    return x2.reshape(bsz, seq, d)
```

```python
import functools
import math

import jax
import jax.numpy as jnp
from jax import lax
from jax.experimental import pallas as pl
from jax.experimental.pallas import tpu as pltpu

D_MODEL = 1024
BLOCK = 128
A_GROUPS = 4
A_WIDTH = 512
B_HEADS = 4
B_QK_DIM = 64
B_V_DIM = 128
C_HEADS = 8
C_HEAD_DIM = 64
C_WIDTH = 512
N_EXPERTS = 16
N_GROUPS = 4
EXPERTS_PER_GROUP = 4
PAIR_LO = (0, 0, 0, 1, 1, 2)
PAIR_HI = (1, 2, 3, 2, 3, 3)
N_BUCKETS = N_GROUPS * len(PAIR_LO)
D_FF_EXPERT = 512
IN_WIDTH = 7168
XE_WIDTH = D_MODEL + 128
EPS = 1e-6

LANES = 128
VMEM_LIMIT = 56 * 1024 * 1024
F32 = jnp.float32
BF16 = jnp.bfloat16
NEG_BIG = -1e30
EXP2_UNDERFLOW = -152.0

LOG2E = 1.4426950408889634

COL_QB, COL_KB, COL_VB = 8, 12, 16
QB_COLS = (2 * A_WIDTH, 2 * A_WIDTH + 512)
QC_COLS = (2 * A_WIDTH + 3 * 512, 2 * A_WIDTH + 4 * 512)


def _nt_dot(a, b):
    return lax.dot_general(a, b, (((1,), (1,)), ((), ())), preferred_element_type=F32)


def _dot(a, b):
    return jnp.dot(a, b, preferred_element_type=F32)


def _sigmoid(x):
    return 0.5 * jnp.tanh(0.5 * x) + 0.5


def _rms(xf, gain):
    return xf * lax.rsqrt(jnp.mean(xf * xf, axis=-1, keepdims=True) + EPS) * gain


def _in_proj_kernel(x_ref, g_ref, w_ref, cs_ref, o_ref, xn_ref):
    @pl.when(pl.program_id(1) == 0)
    def _():
        xn_ref[...] = _rms(x_ref[...], g_ref[...]).astype(BF16)

    o_ref[...] = (_dot(xn_ref[...], w_ref[...]) * cs_ref[...]).astype(BF16)


def _query_scale_row(n):
    col = jnp.arange(n)
    s = jnp.ones((n,), F32)
    s = jnp.where((col >= QB_COLS[0]) & (col < QB_COLS[1]), LOG2E * B_QK_DIM ** -0.5, s)
    s = jnp.where((col >= QC_COLS[0]) & (col < QC_COLS[1]), LOG2E * C_HEAD_DIM ** -0.5, s)
    return s.reshape(1, n)


def in_proj(x2, gain, w_all, layer, tm=2048, tn=1792):
    t, d = x2.shape
    n = w_all.shape[2]
    tm = min(tm, t)
    return pl.pallas_call(
        _in_proj_kernel,
        grid=(t // tm, n // tn),
        in_specs=[
            pl.BlockSpec((tm, d), lambda i, j: (i, 0)),
            pl.BlockSpec((1, d), lambda i, j: (0, 0)),
            pl.BlockSpec((None, d, tn), lambda i, j: (layer, 0, j)),
            pl.BlockSpec((1, tn), lambda i, j: (0, j)),
        ],
        out_specs=pl.BlockSpec((tm, tn), lambda i, j: (i, j)),
        out_shape=jax.ShapeDtypeStruct((t, n), BF16),
        scratch_shapes=[pltpu.VMEM((tm, d), BF16)],
        compiler_params=pltpu.CompilerParams(
            dimension_semantics=("arbitrary", "arbitrary"), vmem_limit_bytes=VMEM_LIMIT),
        name="in_proj",
    )(x2, gain.reshape(1, d), w_all, _query_scale_row(n))


def _diff_attn_kernel(lam_ref, sg_ref, q_ref, k_ref, v_ref, o_ref, m_ref, l_ref, acc_ref, s_ref, p_ref, kmax_ref,
                      j0_ref,
                      *, tq, hps, nq, lambda_init):
    tk = tq
    hp = pl.program_id(1)
    qi = pl.program_id(2)
    nc = tk // LANES
    units = [(hh, mi) for hh in range(hps) for mi in range(2)]
    head_cols = [slice(hh * LANES, (hh + 1) * LANES) for hh in range(hps)]

    def head_slope(h):
        sl = jnp.where(h == 0, 0.25, jnp.where(h == 1, 0.0625, jnp.where(h == 2, 0.015625, 0.00390625)))
        return sl.astype(F32) * LOG2E

    slopes = [head_slope(hp * hps + hh) for hh in range(hps)]

    lane = lax.broadcasted_iota(jnp.int32, (tq, LANES), 1)
    qs = []
    for hh, mi in units:
        q = q_ref[:, head_cols[hh]]
        keep = (lane < B_QK_DIM) if mi == 0 else (lane >= B_QK_DIM)
        qs.append(jnp.where(keep, q, jnp.zeros_like(q)))

    m_ref[...] = jnp.full(m_ref.shape, NEG_BIG, F32)
    l_ref[...] = jnp.zeros(l_ref.shape, F32)
    acc_ref[...] = jnp.zeros(acc_ref.shape, F32)

    kcol = lax.broadcasted_iota(jnp.int32, (1, LANES), 1)
    row_minus_col = lax.broadcasted_iota(jnp.int32, (tq, LANES), 0) - lane
    q0 = qi * tq

    def load(ref, kb, hh):
        return ref[pl.ds(pl.multiple_of(kb * tk, tk), tk), head_cols[hh]]

    def scores(kb):
        kks = [load(k_ref, kb, hh) for hh in range(hps)]
        return [_nt_dot(qs[u], kks[hh]) for u, (hh, _) in enumerate(units)]

    def softmax_step(u, s, kb, masked):
        off = kb * tk - q0
        slope = slopes[units[u][0]]
        cols = []
        for c in range(nc):
            sc = s[:, c * LANES:(c + 1) * LANES] + slope * (kcol + (off + c * LANES)).astype(F32)
            if masked:
                sc = jnp.where(row_minus_col >= (off + c * LANES), sc, NEG_BIG)
            cols.append(sc)
        mx = cols[0]
        for sc in cols[1:]:
            mx = jnp.maximum(mx, sc)
        m_prev = m_ref[u]
        m_new = jnp.maximum(m_prev, jnp.max(mx, axis=-1, keepdims=True))
        alpha = jnp.exp2(m_prev - m_new)
        ps = [jnp.exp2(sc - m_new) for sc in cols]
        psum = ps[0]
        for pc in ps[1:]:
            psum = psum + pc
        l_ref[u] = alpha * l_ref[u] + psum
        acc_ref[u] = alpha * acc_ref[u]
        m_ref[u] = m_new
        return jnp.concatenate([pc.astype(BF16) for pc in ps], axis=1)

    lanes1 = lax.broadcasted_iota(jnp.int32, (1, LANES), 1)

    hs_i = lax.broadcasted_iota(jnp.int32, (2 * LANES, LANES), 0) % LANES
    hs_j = lax.broadcasted_iota(jnp.int32, (2 * LANES, LANES), 1)
    half_sum = jnp.where((hs_i < B_QK_DIM) == (hs_j < B_QK_DIM), 1.0, 0.0).astype(BF16)

    def half_sums(x):
        hi = x.astype(BF16)
        lo = (x - hi.astype(F32)).astype(BF16)
        return _dot(jnp.concatenate([hi, lo], axis=1), half_sum)

    steepest = slopes[0]
    for sl in slopes[1:]:
        steepest = jnp.maximum(steepest, sl)
    far_enough = lambda first_row: steepest * (first_row - tk + 1).astype(F32) > -EXP2_UNDERFLOW

    @pl.when(jnp.logical_and(qi == 0, far_enough(jnp.int32((nq - 1) * tq))))
    def _():
        for hh in range(hps):
            rows = [jnp.zeros((1, LANES), F32) for _ in range(2)]
            for kb in range(nq):
                kk = k_ref[kb * tk:(kb + 1) * tk, head_cols[hh]].astype(F32)
                n2 = jnp.sqrt(jnp.max(half_sums(kk * kk), axis=0, keepdims=True))
                for mi in range(2):
                    at = mi * B_QK_DIM
                    rows[mi] = jnp.where(lanes1 == kb, n2[:, at:at + 1], rows[mi])
            for mi in range(2):
                kmax_ref[2 * hh + mi, 0:1, :] = rows[mi]

    j0_ref[0] = 0

    @pl.when(far_enough(q0))
    def _():
        worst = None
        for hh in range(hps):
            qf = q_ref[:, head_cols[hh]].astype(F32)
            k_own = k_ref[pl.ds(pl.multiple_of(q0, tq), tq), head_cols[hh]].astype(F32)
            score_min = jnp.min(half_sums(qf * k_own), axis=0, keepdims=True)
            norm_max = jnp.sqrt(jnp.max(half_sums(qf * qf), axis=0, keepdims=True))
            top_bias = slopes[hh] * ((lanes1 + 1) * tk - 1 - q0).astype(F32)
            for mi in range(2):
                u = 2 * hh + mi
                at = mi * B_QK_DIM
                gap = norm_max[:, at:at + 1] * kmax_ref[u, 0:1, :] + top_bias - score_min[:, at:at + 1]
                worst = gap if worst is None else jnp.maximum(worst, gap)
        needed = jnp.logical_or(worst >= EXP2_UNDERFLOW, lanes1 >= qi)
        j0_ref[0] = jnp.min(jnp.where(needed, lanes1, LANES))

    j0 = j0_ref[0]

    odd = lax.rem(qi - j0, 2)
    s0 = scores(j0)
    for u in range(len(units)):
        s_ref[odd, u] = s0[u]
        p_ref[odd, u] = jnp.zeros((tq, tk), BF16)

    def pending_pv(j, slot):
        vs = [load(v_ref, jnp.maximum(j - 1, 0), hh) for hh in range(hps)]
        return [_dot(p_ref[slot, u], vs[hh]) for u, (hh, _) in enumerate(units)]

    def step(j, slot):
        nxt = 1 - slot
        pv = pending_pv(j, slot)
        sn = scores(j + 1)
        for u in range(len(units)):
            acc_ref[u] += pv[u]
            s_ref[nxt, u] = sn[u]
        for u in range(len(units)):
            p_ref[nxt, u] = softmax_step(u, s_ref[slot, u], j, False)

    @pl.when(odd == 1)
    def _():
        step(j0, 1)

    def body(t, c):
        j = j0 + odd + 2 * t
        step(j, 0)
        step(j + 1, 1)
        return c

    lax.fori_loop(0, (qi - j0) // 2, body, 0)
    pv = pending_pv(qi, 0)
    for u in range(len(units)):
        acc_ref[u] += pv[u]
    vdiag = [load(v_ref, qi, hh) for hh in range(hps)]
    for u, (hh, _) in enumerate(units):
        p = softmax_step(u, s_ref[0, u], qi, True)
        acc_ref[u] += _dot(p, vdiag[hh])

    lp = lam_ref[...]
    s1 = jnp.sum(lp[0:1] * lp[1:2], axis=-1, keepdims=True)
    s2 = jnp.sum(lp[2:3] * lp[3:4], axis=-1, keepdims=True)
    lam = jnp.exp(s1) - jnp.exp(s2) + lambda_init
    for hh in range(hps):
        l0 = jnp.sum(l_ref[2 * hh], axis=-1, keepdims=True)
        l1 = jnp.sum(l_ref[2 * hh + 1], axis=-1, keepdims=True)
        o = acc_ref[2 * hh] / l0 - lam * (acc_ref[2 * hh + 1] / l1)
        o_ref[:, head_cols[hh]] = (_rms(o, sg_ref[...]) * (1.0 - lambda_init)).astype(BF16)


def diff_attn(proj, lam_params, subln_g, bsz, seq, lambda_init, tq=512, hps=1):
    tq = min(tq, seq)
    nq = seq // tq
    n_units = 2 * hps
    w = hps * LANES
    kern = functools.partial(_diff_attn_kernel, tq=tq, hps=hps, nq=nq, lambda_init=lambda_init)
    return pl.pallas_call(
        kern,
        grid=(bsz, B_HEADS // hps, nq),
        in_specs=[
            pl.BlockSpec((4, B_QK_DIM), lambda b, h, i: (0, 0)),
            pl.BlockSpec((1, B_V_DIM), lambda b, h, i: (0, 0)),
            pl.BlockSpec((tq, w), lambda b, h, i: (b * nq + i, COL_QB // hps + h)),
            pl.BlockSpec((seq, w), lambda b, h, i: (b, COL_KB // hps + h)),
            pl.BlockSpec((seq, w), lambda b, h, i: (b, COL_VB // hps + h)),
        ],
        out_specs=pl.BlockSpec((tq, w), lambda b, h, i: (b * nq + i, h)),
        out_shape=jax.ShapeDtypeStruct((bsz * seq, B_HEADS * B_V_DIM), BF16),
        scratch_shapes=[pltpu.VMEM((n_units, tq, LANES), F32), pltpu.VMEM((n_units, tq, LANES), F32),
                        pltpu.VMEM((n_units, tq, B_V_DIM), F32),
                        pltpu.VMEM((2, n_units, tq, tq), F32), pltpu.VMEM((2, n_units, tq, tq), BF16),
                        pltpu.VMEM((n_units, 8, LANES), F32), pltpu.SMEM((1,), jnp.int32)],
        compiler_params=pltpu.CompilerParams(
            dimension_semantics=("arbitrary", "arbitrary", "arbitrary"), vmem_limit_bytes=VMEM_LIMIT),
        name="diff_attn",
    )(lam_params, subln_g.reshape(1, B_V_DIM), proj, proj, proj)


def _stick_kernel(q_ref, k_ref, v_ref, o_ref, qs_ref, carry_ref, acc_ref, *, tq, n_chains):
    tk = tq
    n_pairs = C_HEADS // 2
    gi = pl.program_id(1)
    lane = lax.broadcasted_iota(jnp.int32, (tq, LANES), 1)
    low = lane < C_HEAD_DIM
    units = [(c, p) for c in range(n_chains) for p in range(n_pairs)]
    for c, p in units:
        q = q_ref[c * tq:(c + 1) * tq, p * LANES:(p + 1) * LANES]
        zero = jnp.zeros_like(q)
        qs_ref[c, p, 0:tq] = jnp.where(low, q, zero)
        qs_ref[c, p, tq:2 * tq] = jnp.where(low, zero, q)
    carry_ref[...] = jnp.zeros(carry_ref.shape, F32)
    acc_ref[...] = jnp.zeros(acc_ref.shape, F32)

    row = lax.broadcasted_iota(jnp.int32, (2 * tq, tk), 0)
    col = lax.broadcasted_iota(jnp.int32, (2 * tq, tk), 1)
    strict = (row % tq) > col
    uj = lax.broadcasted_iota(jnp.int32, (2 * tk, tk + LANES), 0) % tk
    us = lax.broadcasted_iota(jnp.int32, (2 * tk, tk + LANES), 1)
    w2 = jnp.where(jnp.logical_or(uj > us, us >= tk), 1.0, 0.0).astype(BF16)
    vlow = lax.broadcasted_iota(jnp.int32, (tk, LANES), 1) < C_HEAD_DIM

    def step(n, masked):
        kbs = [gi * n_chains + c - n for c in range(n_chains)]
        starts = [pl.multiple_of(jnp.maximum(kb, 0) * tk, tk) for kb in kbs]
        gone = [jnp.where(kb < 0, NEG_BIG, 0.0).astype(F32) for kb in kbs]
        cols = [slice(p * LANES, (p + 1) * LANES) for p in range(n_pairs)]
        zs = [_nt_dot(qs_ref[c, p], k_ref[pl.ds(starts[c], tk), cols[p]]) for c, p in units]
        lbs, rs = [], []
        for z in zs:
            sp = jnp.log2(1.0 + jnp.exp2(-jnp.abs(z)))
            lb = jnp.minimum(z, 0.0) - sp
            lk = lb - z
            if masked:
                lk = jnp.where(strict, lk, 0.0)
            hi = lk.astype(BF16)
            lo = (lk - hi.astype(F32)).astype(BF16)
            lbs.append(lb)
            rs.append(_dot(jnp.concatenate([hi, lo], axis=1), w2))
        alive = None
        for u, (c, p) in enumerate(units):
            carry = carry_ref[c, p]
            a = jnp.exp2(lbs[u] + rs[u][:, :tk] + (carry + gone[c]))
            if masked:
                a = jnp.where(strict, a, 0.0)
            carry = carry + rs[u][:, tk:]
            carry_ref[c, p] = carry
            left = carry + jnp.where(kbs[c] < 1, NEG_BIG, 0.0).astype(F32)
            alive = left if alive is None else jnp.maximum(alive, left)
            ab = a.astype(BF16)
            vv = v_ref[pl.ds(starts[c], tk), cols[p]]
            zero = jnp.zeros_like(vv)
            vst = jnp.concatenate([jnp.where(vlow, vv, zero), jnp.where(vlow, zero, vv)], axis=0)
            acc_ref[c, p] += _dot(jnp.concatenate([ab[:tq], ab[tq:]], axis=1), vst)
        return jnp.max(alive)

    def cond(st):
        _, alive = st
        return alive > EXP2_UNDERFLOW

    def body(st):
        n, _ = st
        return n + 1, step(n, False)

    lax.while_loop(cond, body, (1, step(0, True)))
    for c, p in units:
        o_ref[c * tq:(c + 1) * tq, p * LANES:(p + 1) * LANES] = acc_ref[c, p].astype(BF16)


def stick_attn(proj, bsz, seq, tq=128, n_chains=4):
    n_chains = min(n_chains, seq // tq)
    rows = tq * n_chains
    nq = seq // rows
    kern = functools.partial(_stick_kernel, tq=tq, n_chains=n_chains)
    n_pairs = C_HEADS // 2
    return pl.pallas_call(
        kern,
        grid=(bsz, nq),
        in_specs=[
            pl.BlockSpec((rows, C_WIDTH), lambda b, i: (b * nq + i, 5)),
            pl.BlockSpec((seq, C_WIDTH), lambda b, i: (b, 6)),
            pl.BlockSpec((seq, C_WIDTH), lambda b, i: (b, 7)),
        ],
        out_specs=pl.BlockSpec((rows, C_WIDTH), lambda b, i: (b * nq + i, 0)),
        out_shape=jax.ShapeDtypeStruct((bsz * seq, C_WIDTH), BF16),
        scratch_shapes=[pltpu.VMEM((n_chains, n_pairs, 2 * tq, LANES), BF16),
                        pltpu.VMEM((n_chains, n_pairs, 2 * tq, LANES), F32),
                        pltpu.VMEM((n_chains, n_pairs, tq, LANES), F32)],
        compiler_params=pltpu.CompilerParams(
            dimension_semantics=("arbitrary", "arbitrary"), vmem_limit_bytes=VMEM_LIMIT),
        name="stick_attn",
    )(proj, proj, proj)


BUCKET_ROWS = 32

def _top2_route(sel, aff):
    def top2_sum(a, b, c, d):
        hi1, lo1 = jnp.maximum(a, b), jnp.minimum(a, b)
        hi2, lo2 = jnp.maximum(c, d), jnp.minimum(c, d)
        return jnp.maximum(hi1, hi2) + jnp.maximum(jnp.minimum(hi1, hi2), jnp.maximum(lo1, lo2))

    scores = [top2_sum(*sel[4 * g:4 * g + 4]) for g in range(N_GROUPS)]
    best = jnp.zeros_like(scores[0], dtype=jnp.int32)
    best_score = scores[0]
    for g in range(1, N_GROUPS):
        better = scores[g] > best_score
        best = jnp.where(better, g, best)
        best_score = jnp.where(better, scores[g], best_score)

    def pick(rows, i):
        out = rows[i]
        for g in range(1, N_GROUPS):
            out = jnp.where(best == g, rows[4 * g + i], out)
        return out

    cs = [pick(sel, i) for i in range(EXPERTS_PER_GROUP)]
    ca = [pick(aff, i) for i in range(EXPERTS_PER_GROUP)]

    def argmax_first(vals, exclude=None):
        idx = jnp.full(vals[0].shape, -1, jnp.int32)
        cur = jnp.full(vals[0].shape, -jnp.inf, F32)
        for i, v in enumerate(vals):
            ok = v > cur
            if exclude is not None:
                ok = jnp.logical_and(ok, exclude != i)
            idx = jnp.where(ok, i, idx)
            cur = jnp.where(ok, v, cur)
        return idx

    i1 = argmax_first(cs)
    i2 = argmax_first(cs, exclude=i1)

    def take(vals, idx):
        out = vals[0]
        for i in range(1, len(vals)):
            out = jnp.where(idx == i, vals[i], out)
        return out

    w1, w2 = take(ca, i1), take(ca, i2)
    tot = w1 + w2
    w1, w2 = w1 / tot, w2 / tot
    e1 = best * EXPERTS_PER_GROUP + i1
    e2 = best * EXPERTS_PER_GROUP + i2
    gates = [jnp.where(e1 == e, w1, 0.0) + jnp.where(e2 == e, w2, 0.0) for e in range(N_EXPERTS)]
    lo, hi = jnp.minimum(i1, i2), jnp.maximum(i1, i2)
    pair = jnp.where(lo == 0, 0, jnp.where(lo == 1, 3, 5)) + (hi - lo - 1)
    return gates, best * len(PAIR_LO) + pair


def _merge_kernel(x_ref, u_ref, v_ref, g0_ref, g1_ref, g2_ref, yb_ref, yc_ref,
                  ws_ref, bst_ref, gv_ref, wa_ref, wb_ref, wc_ref, wo_ref, ng_ref, rwt_ref, rb_ref,
                  h_ref, xe_ref, grp_ref, rank_ref, cnt_ref, ya_ref, run_ref, *, tm):
    u = jax.nn.gelu(u_ref[...].astype(F32))
    v = _rms(jax.nn.gelu(v_ref[...].astype(F32)), gv_ref[...]).astype(BF16)

    r = lax.broadcasted_iota(jnp.int32, (BLOCK, BLOCK), 0)
    c = lax.broadcasted_iota(jnp.int32, (BLOCK, BLOCK), 1)
    causal = r >= c
    for g in range(A_GROUPS):
        w = jnp.where(causal, ws_ref[g], 0.0).astype(BF16)
        bias = bst_ref[:, g:g + 1]
        cols = slice(g * LANES, (g + 1) * LANES)
        for ch in range(tm // BLOCK):
            rows = slice(ch * BLOCK, (ch + 1) * BLOCK)
            mixed = _dot(w, v[rows, cols]) + bias
            ya_ref[rows, cols] = (u[rows, cols] * mixed).astype(BF16)

    merged = _sigmoid(g0_ref[...].astype(F32)) * _dot(ya_ref[...], wa_ref[...])
    merged += _sigmoid(g1_ref[...].astype(F32)) * _dot(yb_ref[...], wb_ref[...])
    merged += _sigmoid(g2_ref[...].astype(F32)) * _dot(yc_ref[...], wc_ref[...])
    h = x_ref[...] + _dot(merged.astype(BF16), wo_ref[...])
    h_ref[...] = h

    xn = _rms(h, ng_ref[...])
    xe_ref[:, :D_MODEL] = xn

    xh = xn.astype(BF16)
    xl = (xn - xh.astype(F32)).astype(BF16)
    rw = rwt_ref[...]
    rh = rw.astype(BF16)
    rl = (rw - rh.astype(F32)).astype(BF16)
    logits = _nt_dot(rh, xh) + (_nt_dot(rh, xl) + _nt_dot(rl, xh))
    aff = jax.nn.sigmoid(logits)
    sel = aff + rb_ref[...]
    gates, bucket = _top2_route([sel[e:e + 1] for e in range(N_EXPERTS)],
                              [aff[e:e + 1] for e in range(N_EXPERTS)])
    gate_rows = jnp.concatenate(gates + [jnp.zeros((LANES - N_EXPERTS, tm), F32)], axis=0)
    xe_ref[:, D_MODEL:] = gate_rows.T

    @pl.when(pl.program_id(0) == 0)
    def _():
        run_ref[...] = jnp.zeros(run_ref.shape, F32)

    bid = lax.broadcasted_iota(jnp.int32, (BUCKET_ROWS, tm), 0)
    member = (bid == bucket).astype(F32)
    before = lax.broadcasted_iota(jnp.int32, (tm, tm), 0) < lax.broadcasted_iota(jnp.int32, (tm, tm), 1)
    prefix = _dot(member.astype(BF16), before.astype(BF16))
    run = run_ref[...]
    rank = jnp.sum(member * (prefix + run[:, 0:1]), axis=0, keepdims=True)
    rank_ref[...] = rank.astype(jnp.int32)
    grp_ref[...] = bucket
    run = run + jnp.sum(member, axis=-1, keepdims=True)
    run_ref[...] = run
    cnt_ref[...] = run


def merge(x2, proj, yb, yc, w_s, b_s_t, g_v, wa, wb, wc, wo, layer, norm_g, router_wt, router_b, tm=512):
    t, d = x2.shape
    tm = min(tm, t)
    row = lambda i: (i, 0)
    const2 = lambda i: (0, 0)
    of_layer = lambda i: (layer, 0, 0)
    kern = functools.partial(_merge_kernel, tm=tm)
    return pl.pallas_call(
        kern,
        grid=(t // tm,),
        in_specs=[
            pl.BlockSpec((tm, d), row),
            pl.BlockSpec((tm, A_WIDTH), lambda i: (i, 0)),
            pl.BlockSpec((tm, A_WIDTH), lambda i: (i, 1)),
            pl.BlockSpec((tm, d), lambda i: (i, 4)),
            pl.BlockSpec((tm, d), lambda i: (i, 5)),
            pl.BlockSpec((tm, d), lambda i: (i, 6)),
            pl.BlockSpec((tm, A_WIDTH), row),
            pl.BlockSpec((tm, C_WIDTH), row),
            pl.BlockSpec((A_GROUPS, BLOCK, BLOCK), lambda i: (0, 0, 0)),
            pl.BlockSpec((BLOCK, A_GROUPS), const2),
            pl.BlockSpec((1, A_WIDTH), const2),
            pl.BlockSpec((None, A_WIDTH, d), of_layer),
            pl.BlockSpec((None, A_WIDTH, d), of_layer),
            pl.BlockSpec((None, C_WIDTH, d), of_layer),
            pl.BlockSpec((None, d, d), of_layer),
            pl.BlockSpec((1, d), const2),
            pl.BlockSpec((N_EXPERTS, d), const2),
            pl.BlockSpec((N_EXPERTS, 1), const2),
        ],
        out_specs=[
            pl.BlockSpec((tm, d), row),
            pl.BlockSpec((tm, XE_WIDTH), row),
            pl.BlockSpec((1, tm), lambda i: (0, i)),
            pl.BlockSpec((1, tm), lambda i: (0, i)),
            pl.BlockSpec((BUCKET_ROWS, LANES), const2),
        ],
        out_shape=[
            jax.ShapeDtypeStruct((t, d), F32),
            jax.ShapeDtypeStruct((t, XE_WIDTH), F32),
            jax.ShapeDtypeStruct((1, t), jnp.int32),
            jax.ShapeDtypeStruct((1, t), jnp.int32),
            jax.ShapeDtypeStruct((BUCKET_ROWS, LANES), F32),
        ],
        scratch_shapes=[pltpu.VMEM((tm, A_WIDTH), BF16), pltpu.VMEM((BUCKET_ROWS, LANES), F32)],
        compiler_params=pltpu.CompilerParams(
            dimension_semantics=("arbitrary",), vmem_limit_bytes=VMEM_LIMIT),
        name="merge",
    )(x2, proj, proj, proj, proj, proj, yb, yc, w_s, b_s_t, g_v.reshape(1, A_WIDTH),
      wa, wb, wc, wo, norm_g.reshape(1, d), router_wt, router_b.reshape(N_EXPERTS, 1))


def _route_tables(cnt, bkt, rank, t, rt):
    nb = N_BUCKETS
    cnt = cnt[:nb, 0].astype(jnp.int32)
    padded = (cnt + rt - 1) // rt * rt
    off_end = jnp.cumsum(padded)
    off = off_end - padded
    b = bkt.reshape(t)
    pos = rank.reshape(t) + off[b]
    starts = jnp.arange(t // rt + nb, dtype=jnp.int32) * rt
    tile_bucket = jnp.minimum(jnp.sum((starts[:, None] >= off_end[None, :]).astype(jnp.int32), axis=1), nb - 1)
    first = (tile_bucket // len(PAIR_LO)) * EXPERTS_PER_GROUP
    pair = tile_bucket % len(PAIR_LO)
    tile_ea = first + jnp.asarray(PAIR_LO, jnp.int32)[pair]
    tile_eb = first + jnp.asarray(PAIR_HI, jnp.int32)[pair]
    tile_valid = (starts < off_end[nb - 1]).astype(jnp.int32)
    last_tile = jnp.maximum(off_end[nb - 1] // rt - 1, 0).reshape(1)
    tail = off_end[nb - 1] + jnp.arange(nb, dtype=jnp.int32) * rt
    fill_row = jnp.concatenate([jnp.maximum(off_end - rt, 0), jnp.minimum(tail, t + (nb - 1) * rt)])
    fill_on = jnp.concatenate([padded > 0, tail < t + nb * rt]).astype(jnp.int32)
    return pos, tile_ea, tile_eb, tile_valid, last_tile, fill_row, fill_on


ROW_DMA_UNROLL = 16


def _row_copy(src_ref, src_row, dst_ref, dst_row, sem):
    return pltpu.make_async_copy(src_ref.at[pl.ds(src_row, 1)], dst_ref.at[pl.ds(dst_row, 1)], sem)


def _dispatch_kernel(pos_ref, fill_row_ref, fill_on_ref, xe_ref, xs_ref, zero_ref, sem, *, tm, rt):
    base = pl.program_id(0) * tm

    @pl.when(pl.program_id(0) == 0)
    def _():
        zero_ref[...] = jnp.zeros(zero_ref.shape, F32)

        def fill(k):
            row = pl.multiple_of(fill_row_ref[k], rt)
            return pltpu.make_async_copy(zero_ref, xs_ref.at[pl.ds(row, rt)], sem)

        for k in range(2 * N_BUCKETS):
            @pl.when(fill_on_ref[k] == 1)
            def _():
                fill(k).start()
        for k in range(2 * N_BUCKETS):
            @pl.when(fill_on_ref[k] == 1)
            def _():
                fill(k).wait()

    def issue(r, c):
        _row_copy(xe_ref, r, xs_ref, pos_ref[base + r], sem).start()
        return c

    lax.fori_loop(0, tm, issue, 0, unroll=ROW_DMA_UNROLL)
    pltpu.make_async_copy(xe_ref, xs_ref.at[pl.ds(0, tm)], sem).wait()


def moe_dispatch(pos, fill_row, fill_on, xe, t_pad, rt, tm=1024):
    t, w = xe.shape
    tm = min(tm, t)
    kern = functools.partial(_dispatch_kernel, tm=tm, rt=rt)
    return pl.pallas_call(
        kern,
        grid_spec=pltpu.PrefetchScalarGridSpec(
            num_scalar_prefetch=3,
            grid=(t // tm,),
            in_specs=[pl.BlockSpec((tm, w), lambda i, *_: (i, 0))],
            out_specs=pl.BlockSpec(memory_space=pl.ANY),
            scratch_shapes=[pltpu.VMEM((rt, w), F32), pltpu.SemaphoreType.DMA],
        ),
        out_shape=jax.ShapeDtypeStruct((t_pad, w), F32),
        compiler_params=pltpu.CompilerParams(
            dimension_semantics=("arbitrary",), vmem_limit_bytes=VMEM_LIMIT),
        name="moe_dispatch",
    )(pos, fill_row, fill_on, xe)


def _moe_ffn_kernel(ea_ref, eb_ref, tv_ref, lt_ref, xs_ref, w1a_ref, w3a_ref, w2a_ref, w1b_ref, w3b_ref, w2b_ref,
                    ys_ref):
    del lt_ref
    i = pl.program_id(0)
    valid = tv_ref[i] == 1

    @pl.when(valid)
    def _():
        x = xs_ref[:, :D_MODEL].astype(BF16)
        ext = xs_ref[:, D_MODEL:]
        lane = lax.broadcasted_iota(jnp.int32, ext.shape, 1)

        def up(w1_ref, w3_ref):
            return _dot(x, w1_ref[...]), _dot(x, w3_ref[...])

        def down(expert, ab, w2_ref):
            gcol = jnp.sum(jnp.where(lane == expert, ext, 0.0), axis=-1, keepdims=True)
            hh = ((jax.nn.silu(ab[0]) * ab[1]) * gcol).astype(BF16)
            return _dot(hh, w2_ref[...])

        ab_a = up(w1a_ref, w3a_ref)
        ab_b = up(w1b_ref, w3b_ref)
        ys_ref[...] = down(ea_ref[i], ab_a, w2a_ref) + down(eb_ref[i], ab_b, w2b_ref)

    @pl.when(jnp.logical_not(valid))
    def _():
        ys_ref[...] = jnp.zeros(ys_ref.shape, F32)


def moe_ffn(tile_ea, tile_eb, tile_valid, last_tile, xs, w1, w3, w2, layer, rt):
    t_pad, w = xs.shape
    d = D_MODEL
    of_a = lambda i, ea, eb, tv, lt: (layer, ea[i], 0, 0)
    of_b = lambda i, ea, eb, tv, lt: (layer, eb[i], 0, 0)
    up_block, down_block = (None, None, d, D_FF_EXPERT), (None, None, D_FF_EXPERT, d)
    return pl.pallas_call(
        _moe_ffn_kernel,
        grid_spec=pltpu.PrefetchScalarGridSpec(
            num_scalar_prefetch=4,
            grid=(t_pad // rt,),
            in_specs=[pl.BlockSpec((rt, w), lambda i, ea, eb, tv, lt: (jnp.minimum(i, lt[0]), 0)),
                      pl.BlockSpec(up_block, of_a), pl.BlockSpec(up_block, of_a), pl.BlockSpec(down_block, of_a),
                      pl.BlockSpec(up_block, of_b), pl.BlockSpec(up_block, of_b), pl.BlockSpec(down_block, of_b)],
            out_specs=pl.BlockSpec((rt, d), lambda i, ea, eb, tv, lt: (i, 0)),
        ),
        out_shape=jax.ShapeDtypeStruct((t_pad, d), F32),
        compiler_params=pltpu.CompilerParams(
            dimension_semantics=("arbitrary",), vmem_limit_bytes=VMEM_LIMIT),
        name="moe_ffn",
    )(tile_ea, tile_eb, tile_valid, last_tile, xs, w1, w3, w2, w1, w3, w2)


def _combine_kernel(pos_ref, h_ref, ys_ref, fg_ref, o_ref, ybuf_ref, sem, *, tm, final_norm):
    base = pl.program_id(0) * tm

    def issue(r, c):
        _row_copy(ys_ref, pos_ref[base + r], ybuf_ref, r, sem).start()
        return c

    lax.fori_loop(0, tm, issue, 0, unroll=ROW_DMA_UNROLL)
    pltpu.make_async_copy(ys_ref.at[pl.ds(0, tm)], ybuf_ref, sem).wait()
    y = h_ref[...] + ybuf_ref[...]
    if final_norm:
        y = _rms(y, fg_ref[...])
    o_ref[...] = y


def moe_combine(pos, h, ys, final_gain, final_norm, tm=1024):
    t, d = h.shape
    tm = min(tm, t)
    kern = functools.partial(_combine_kernel, tm=tm, final_norm=final_norm)
    return pl.pallas_call(
        kern,
        grid_spec=pltpu.PrefetchScalarGridSpec(
            num_scalar_prefetch=1,
            grid=(t // tm,),
            in_specs=[pl.BlockSpec((tm, d), lambda i, pos: (i, 0)),
                      pl.BlockSpec(memory_space=pl.ANY),
                      pl.BlockSpec((1, d), lambda i, pos: (0, 0))],
            out_specs=pl.BlockSpec((tm, d), lambda i, pos: (i, 0)),
            scratch_shapes=[pltpu.VMEM((tm, d), F32), pltpu.SemaphoreType.DMA],
        ),
        out_shape=jax.ShapeDtypeStruct((t, d), F32),
        compiler_params=pltpu.CompilerParams(
            dimension_semantics=("arbitrary",), vmem_limit_bytes=VMEM_LIMIT),
        name="moe_combine",
    )(pos, h, ys, final_gain.reshape(1, d))


def grouped_moe(h, xe, grp, rank, cnt, w1, w3, w2, layer, final_gain, final_norm, rt=512):
    t = h.shape[0]
    rt = min(rt, t)
    pos, tile_ea, tile_eb, tile_valid, last_tile, fill_row, fill_on = _route_tables(cnt, grp, rank, t, rt)
    xs = moe_dispatch(pos, fill_row, fill_on, xe, t + N_BUCKETS * rt, rt)
    ys = moe_ffn(tile_ea, tile_eb, tile_valid, last_tile, xs, w1, w3, w2, layer, rt)
    return moe_combine(pos, h, ys, final_gain, final_norm)


def kernel(x, norm_mix_gain, w_in, gmlp_w_s, gmlp_b_s, gmlp_v_gain, diff_lambda, diff_subln_gain,
           w_up_a, w_up_b, w_up_c, w_out, norm_ffn_gain, router_w, router_bias,
           moe_w1, moe_w3, moe_w2, final_gain):
    bsz, seq, d = x.shape
    depth = w_in.shape[0]
    x2 = x.reshape(bsz * seq, d)
    router_wt = router_w.T
    w_in, w_up_a, w_up_b, w_up_c, w_out, moe_w1, moe_w3, moe_w2 = (
        w.astype(BF16) for w in (w_in, w_up_a, w_up_b, w_up_c, w_out, moe_w1, moe_w3, moe_w2))
    for l in range(depth):
        lambda_init = 0.8 - 0.6 * math.exp(-0.3 * l)
        proj = in_proj(x2, norm_mix_gain[l], w_in, l)
        yb = diff_attn(proj, diff_lambda[l], diff_subln_gain[l], bsz, seq, lambda_init)
        yc = stick_attn(proj, bsz, seq)
        h, xe, grp, rank, cnt = merge(x2, proj, yb, yc, gmlp_w_s[l], gmlp_b_s[l].T, gmlp_v_gain[l],
                                      w_up_a, w_up_b, w_up_c, w_out, l, norm_ffn_gain[l], router_wt, router_bias)
        x2 = grouped_moe(h, xe, grp, rank, cnt, moe_w1, moe_w3, moe_w2, l, final_gain,
                         final_norm=(l == depth - 1))
    return x2.reshape(bsz, seq, d)
```
